```python
import jax, jax.numpy as jnp
from jax import lax
import numpy as np

D_MODEL = 1024
BATCH = 4
SEQ = 8192
DEPTH = 1

RWKV_HEADS = 8
RWKV_HEAD_DIM = 64
RWKV_WIDTH = RWKV_HEADS * RWKV_HEAD_DIM
DECAY_LORA = 64
ICLR_LORA = 64
GATE_LORA = 160
LNX_EPS = 64e-5
MOBA_HEADS = 8
MOBA_HEAD_DIM = 64
MOBA_WIDTH = MOBA_HEADS * MOBA_HEAD_DIM
MOBA_BLOCK = 256
MOBA_TOPK = 3
MOBA_Q_CHUNK = 64
N_BRANCH = 2
BRANCH_WIDTH = 512
RWKV_COLS = 3 * RWKV_WIDTH + DECAY_LORA + ICLR_LORA + GATE_LORA
MOBA_COLS = 3 * MOBA_WIDTH
GATE_COLS = N_BRANCH * D_MODEL
IN_COLS = RWKV_COLS + MOBA_COLS + GATE_COLS
RWKV_SPLITS = (RWKV_WIDTH, 2 * RWKV_WIDTH, 3 * RWKV_WIDTH,
               3 * RWKV_WIDTH + DECAY_LORA, 3 * RWKV_WIDTH + DECAY_LORA + ICLR_LORA)
PEER_HEADS = 8
PEER_N_KEYS = 128
PEER_N_EXPERTS = PEER_N_KEYS * PEER_N_KEYS
PEER_KEY_DIM = 128
PEER_TOPK = 16
PEER_TOKEN_CHUNK = 128

NORM_EPS = 1e-6
NEG_INF = -1e30

kernel_name = "rwkv7_moba_gated_peer_adaln_block"


def rms_norm(x, gain, eps=NORM_EPS):
    xf = x.astype(jnp.float32)
    y = xf * lax.rsqrt(jnp.mean(xf * xf, axis=-1, keepdims=True) + eps)
    return (y * gain.astype(jnp.float32)).astype(x.dtype)


def rwkv7_scan(r, w, k, v, a, b):
    B, S, H, N = r.shape

    def step(state, inp):
        r_t, w_t, k_t, v_t, a_t, b_t = inp
        sa = jnp.einsum('bhvk,bhk->bhv', state, a_t)
        state = (state * w_t[:, :, None, :] + sa[..., None] * b_t[:, :, None, :]
                 + v_t[..., None] * k_t[:, :, None, :])
        return state, jnp.einsum('bhvk,bhk->bhv', state, r_t)

    xs = tuple(jnp.swapaxes(t, 0, 1) for t in (r, w, k, v, a, b))
    state0 = jnp.zeros((B, H, N, N), jnp.float32)
    _, out = lax.scan(step, state0, xs)
    return jnp.swapaxes(out, 0, 1)


def rwkv7_branch(p, mu, w0, w2_decay, a0, a2_iclr, g2_gate, k_k, k_a, r_k, lnx_w, lnx_b):
    B, S, _ = p.shape
    H, N = RWKV_HEADS, RWKV_HEAD_DIM
    f32 = jnp.float32
    p_prev = jnp.pad(p, ((0, 0), (1, 0), (0, 0)))[:, :-1]
    xm = p + (p_prev - p) * mu
    r, k, v, wl, al, gl = jnp.split(xm, RWKV_SPLITS, axis=-1)
    w_log = -jax.nn.softplus(-(w0 + jnp.tanh(wl) @ w2_decay).astype(f32)) - 0.5
    decay = jnp.exp(-jnp.exp(w_log))
    a = jax.nn.sigmoid((a0 + al @ a2_iclr).astype(f32))
    g = jax.nn.sigmoid(gl) @ g2_gate
    r, k, v = r.astype(f32), k.astype(f32), v.astype(f32)

    def heads(t):
        return t.reshape(B, S, H, N)

    kk = heads(k * k_k)
    kk = kk / jnp.maximum(jnp.sqrt(jnp.sum(kk * kk, axis=-1, keepdims=True)), 1e-12)
    k = k * (1.0 + (a - 1.0) * k_a)
    out = rwkv7_scan(heads(r), heads(decay), heads(k), heads(v), -kk, kk * heads(a))
    mean = jnp.mean(out, axis=-1, keepdims=True)
    var = jnp.mean(jnp.square(out - mean), axis=-1, keepdims=True)
    out = ((out - mean) * lax.rsqrt(var + LNX_EPS)).reshape(B, S, RWKV_WIDTH) * lnx_w + lnx_b
    bonus = jnp.sum(heads(r) * heads(k) * r_k, axis=-1, keepdims=True) * heads(v)
    y = (out + bonus.reshape(B, S, RWKV_WIDTH)) * g
    return y.astype(p.dtype)


def moba_branch(q, k, v, q_norm_g, k_norm_g):
    B, S, _ = q.shape
    H, Dh, BL, QC = MOBA_HEADS, MOBA_HEAD_DIM, MOBA_BLOCK, MOBA_Q_CHUNK
    out_dtype = q.dtype

    def heads(t):
        return t.reshape(B, S, H, Dh).transpose(0, 2, 1, 3).astype(jnp.float32)

    q = rms_norm(heads(q), q_norm_g) * (Dh ** -0.5)
    k = rms_norm(heads(k), k_norm_g)
    v = heads(v)
    nb = -(-S // BL)
    pad = nb * BL - S
    kp = jnp.pad(k, ((0, 0), (0, 0), (0, pad), (0, 0)))
    vp = jnp.pad(v, ((0, 0), (0, 0), (0, pad), (0, 0)))
    kblk = kp.reshape(B, H, nb, BL, Dh)
    vblk = vp.reshape(B, H, nb, BL, Dh)
    kmean = jnp.mean(kblk, axis=3)
    n_sel = min(MOBA_TOPK, nb)
    qblk = jnp.arange(S) // BL
    gate = jnp.einsum('bhsd,bhnd->bhsn', q, kmean)
    gate = jnp.where(jnp.arange(nb)[None, :] < qblk[:, None], gate, NEG_INF)
    _, sel = lax.top_k(gate, n_sel)
    bi = jnp.arange(B)[:, None, None, None]
    hi = jnp.arange(H)[None, :, None, None]

    def chunk(ci):
        t0 = ci * QC
        blk = t0 // BL
        qc = lax.dynamic_slice_in_dim(q, t0, QC, axis=2)
        selc = lax.dynamic_slice_in_dim(sel, t0, QC, axis=2)
        kg = kblk[bi, hi, selc]
        vg = vblk[bi, hi, selc]
        k_own = lax.dynamic_slice_in_dim(kp, blk * BL, BL, axis=2)
        v_own = lax.dynamic_slice_in_dim(vp, blk * BL, BL, axis=2)
        s_sel = jnp.einsum('bhqd,bhqjkd->bhqjk', qc, kg)
        s_sel = jnp.where((jnp.arange(n_sel) < blk)[:, None], s_sel, NEG_INF)
        s_own = jnp.einsum('bhqd,bhkd->bhqk', qc, k_own)
        qpos = t0 + jnp.arange(QC)
        kpos = blk * BL + jnp.arange(BL)
        s_own = jnp.where(kpos[None, :] <= qpos[:, None], s_own, NEG_INF)
        scores = jnp.concatenate([s_sel.reshape(B, H, QC, n_sel * BL), s_own], axis=-1)
        p = jax.nn.softmax(scores, axis=-1)
        p_sel = p[..., :n_sel * BL].reshape(B, H, QC, n_sel, BL)
        p_own = p[..., n_sel * BL:]
        return (jnp.einsum('bhqjk,bhqjkd->bhqd', p_sel, vg)
                + jnp.einsum('bhqk,bhkd->bhqd', p_own, v_own))

    out = lax.map(chunk, jnp.arange(S // QC))
    out = out.transpose(1, 0, 3, 2, 4).reshape(B, S, H * Dh)
    return out.astype(out_dtype)


def peer_ffn(h, w_peer_q, sub_keys, u_tab, v_tab):
    B, S, D = h.shape
    HP, K = PEER_HEADS, PEER_TOPK
    q = (h @ w_peer_q).reshape(B, S, HP, 2, PEER_KEY_DIM).astype(jnp.float32)
    s1 = jnp.einsum('bshd,hnd->bshn', q[..., 0, :], sub_keys[0].astype(jnp.float32))
    s2 = jnp.einsum('bshd,hnd->bshn', q[..., 1, :], sub_keys[1].astype(jnp.float32))
    v1, i1 = lax.top_k(s1, K)
    v2, i2 = lax.top_k(s2, K)
    comb = (v1[..., :, None] + v2[..., None, :]).reshape(B, S, HP, K * K)
    sc, ci = lax.top_k(comb, K)
    e1 = jnp.take_along_axis(i1, ci // K, axis=-1)
    e2 = jnp.take_along_axis(i2, ci % K, axis=-1)
    idx = e1 * PEER_N_KEYS + e2
    gw = jax.nn.softmax(sc, axis=-1)
    n_ch = (B * S) // PEER_TOKEN_CHUNK
    hc = h.reshape(n_ch, PEER_TOKEN_CHUNK, D)
    idxc = idx.reshape(n_ch, PEER_TOKEN_CHUNK, HP * K)
    gwc = gw.reshape(n_ch, PEER_TOKEN_CHUNK, HP * K).astype(h.dtype)

    def chunk(args):
        h_c, i_c, g_c = args
        act = jax.nn.gelu(jnp.einsum('td,ted->te', h_c, u_tab[i_c]), approximate=False) * g_c
        return jnp.einsum('te,ted->td', act, v_tab[i_c])

    out = lax.map(chunk, (hc, idxc, gwc))
    return out.reshape(B, S, D)


def setup_inputs(seed: int = 0) -> dict:
    key = jax.random.key(seed)
    ks = jax.random.split(key, 32)
    nrm = jax.random.normal
    D = D_MODEL
    return {
        "x": nrm(ks[0], (BATCH, SEQ, D), jnp.float32),
        "c": nrm(ks[1], (BATCH, D), jnp.float32),
        "w_ada": nrm(ks[2], (D, 6 * D), jnp.float32) * (0.5 * D ** -0.5),
        "b_ada": nrm(ks[3], (6 * D,), jnp.float32) * 0.01,
        "norm1_g": 1.0 + 0.02 * nrm(ks[4], (D,), jnp.float32),
        "w_in": nrm(ks[5], (D, IN_COLS), jnp.float32) * D ** -0.5,
        "mu_rwkv": jax.random.uniform(ks[6], (RWKV_COLS,), jnp.float32),
        "w0": jax.random.uniform(ks[7], (RWKV_WIDTH,), jnp.float32, -6.0, 0.0),
        "w2_decay": nrm(ks[8], (DECAY_LORA, RWKV_WIDTH), jnp.float32) * DECAY_LORA ** -0.5,
        "a0": 0.1 * nrm(ks[9], (RWKV_WIDTH,), jnp.float32),
        "a2_iclr": nrm(ks[10], (ICLR_LORA, RWKV_WIDTH), jnp.float32) * ICLR_LORA ** -0.5,
        "g2_gate": nrm(ks[11], (GATE_LORA, RWKV_WIDTH), jnp.float32) * GATE_LORA ** -0.5,
        "k_k": 0.85 + 0.02 * nrm(ks[12], (RWKV_WIDTH,), jnp.float32),
        "k_a": 1.0 + 0.02 * nrm(ks[13], (RWKV_WIDTH,), jnp.float32),
        "r_k": 0.1 * nrm(ks[14], (RWKV_HEADS, RWKV_HEAD_DIM), jnp.float32),
        "lnx_w": 1.0 + 0.02 * nrm(ks[15], (RWKV_WIDTH,), jnp.float32),
        "lnx_b": 0.01 * nrm(ks[16], (RWKV_WIDTH,), jnp.float32),
        "q_norm_g": 1.0 + 0.02 * nrm(ks[17], (MOBA_HEAD_DIM,), jnp.float32),
        "k_norm_g": 1.0 + 0.02 * nrm(ks[18], (MOBA_HEAD_DIM,), jnp.float32),
        "w_branch": nrm(ks[19], (N_BRANCH, BRANCH_WIDTH, D), jnp.float32) * BRANCH_WIDTH ** -0.5,
        "w_out": nrm(ks[20], (D, D), jnp.float32) * D ** -0.5,
        "norm2_g": 1.0 + 0.02 * nrm(ks[21], (D,), jnp.float32),
        "w_peer_q": nrm(ks[22], (D, PEER_HEADS * 2 * PEER_KEY_DIM), jnp.float32) * D ** -0.5,
        "peer_sub_keys": nrm(ks[23], (2, PEER_HEADS, PEER_N_KEYS, PEER_KEY_DIM), jnp.float32) * PEER_KEY_DIM ** -0.5,
        "peer_u": nrm(ks[24], (PEER_N_EXPERTS, D), jnp.float32) * D ** -0.5,
        "peer_v": nrm(ks[25], (PEER_N_EXPERTS, D), jnp.float32) * (PEER_HEADS * PEER_TOPK) ** -0.5,
    }


def reference(x, c, w_ada, b_ada, norm1_g, w_in, mu_rwkv, w0, w2_decay, a0, a2_iclr, g2_gate,
              k_k, k_a, r_k, lnx_w, lnx_b, q_norm_g, k_norm_g, w_branch, w_out, norm2_g,
              w_peer_q, peer_sub_keys, peer_u, peer_v):
    B, S, D = x.shape
    for _ in range(DEPTH):
        ada = jax.nn.silu(c) @ w_ada + b_ada
        shift1, scale1, gate1, shift2, scale2, gate2 = jnp.split(ada[:, None, :], 6, axis=-1)
        h = rms_norm(x, norm1_g) * (1.0 + scale1) + shift1
        proj = h @ w_in
        p_rwkv, p_moba, p_gate = jnp.split(proj, (RWKV_COLS, RWKV_COLS + MOBA_COLS), axis=-1)
        y_a = rwkv7_branch(p_rwkv, mu_rwkv, w0, w2_decay, a0, a2_iclr, g2_gate,
                           k_k, k_a, r_k, lnx_w, lnx_b)
        q, k, v = jnp.split(p_moba, 3, axis=-1)
        y_b = moba_branch(q, k, v, q_norm_g, k_norm_g)
        ys = jnp.einsum('bsnc,ncd->bsnd', jnp.stack([y_a, y_b], axis=2), w_branch)
        gates = jax.nn.sigmoid(p_gate.reshape(B, S, N_BRANCH, D))
        mixed = jnp.sum(gates * ys, axis=2) @ w_out
        x = x + gate1 * mixed
        h2 = rms_norm(x, norm2_g) * (1.0 + scale2) + shift2
        x = x + gate2 * peer_ffn(h2, w_peer_q, peer_sub_keys, peer_u, peer_v)
    return x
```

```python
import functools

import jax
import jax.numpy as jnp
from jax import lax
from jax.experimental import pallas as pl
from jax.experimental.pallas import tpu as pltpu

F32 = jnp.float32
BF16 = jnp.bfloat16
I32 = jnp.int32

LANES = 128
SUBLANES = 8
VMEM_LIMIT_BYTES = 56 * 1024 * 1024

NORM_EPS = 1e-6
LNX_EPS = 64e-5
NEG_INF = -1e30

HEAD_DIM = 64
N_HEADS = 8
WIDTH = N_HEADS * HEAD_DIM
RWKV_CHUNK = 64
RWKV_TILE = 128
RWKV_PCOLS = 2048
MOBA_BLOCK = 256
MOBA_TOPK = 3
PEER_HEADS = 8
PEER_KEYS = 128
PEER_TOPK = 16
PEER_PICKS = PEER_HEADS * PEER_TOPK


def _mm(a, b):
    return jnp.dot(a.astype(BF16), b.astype(BF16), preferred_element_type=F32)


def _mm_nt(a, b):
    return lax.dot_general(a.astype(BF16), b.astype(BF16), (((1,), (1,)), ((), ())),
                           preferred_element_type=F32)


def _mm_tn(a, b):
    return lax.dot_general(a.astype(BF16), b.astype(BF16), (((0,), (0,)), ((), ())),
                           preferred_element_type=F32)


def _mmh(a, b):
    return jnp.dot(a, b, preferred_element_type=F32, precision=lax.Precision.HIGHEST)


def _head_ones(n):
    r = lax.broadcasted_iota(I32, (n, n), 0) // HEAD_DIM
    c = lax.broadcasted_iota(I32, (n, n), 1) // HEAD_DIM
    return (r == c).astype(F32)


def _sigmoid(x):
    return 1.0 / (1.0 + jnp.exp(-x))


def _ada_kernel(c_ref, w_ref, b_ref, o_ref):
    c = c_ref[...]
    o_ref[...] = _mmh(c * _sigmoid(c), w_ref[...]) + b_ref[...]


def _ada_call(c, w_ada, b_ada):
    bsz, d = c.shape
    n = w_ada.shape[1]
    tn = 1536
    return pl.pallas_call(
        _ada_kernel,
        grid=(n // tn,),
        in_specs=[pl.BlockSpec((bsz, d), lambda j: (0, 0)),
                  pl.BlockSpec((d, tn), lambda j: (0, j)),
                  pl.BlockSpec((1, tn), lambda j: (0, j))],
        out_specs=pl.BlockSpec((bsz, tn), lambda j: (0, j)),
        out_shape=jax.ShapeDtypeStruct((bsz, n), F32),
        compiler_params=pltpu.CompilerParams(dimension_semantics=("arbitrary",),
                                             vmem_limit_bytes=VMEM_LIMIT_BYTES),
        name="ada",
    )(c, w_ada, b_ada.reshape(1, n))


def _rwkv_kernel(p_ref, pprev_ref, mu_ref, vec_ref, w2_ref, a2_ref, g2_ref, y_ref, st_ref):
    i = pl.program_id(1)
    tile = p_ref.shape[1]
    L = RWKV_CHUNK

    @pl.when(i == 0)
    def _():
        st_ref[...] = jnp.zeros_like(st_ref)

    p = p_ref[0]
    prev_last = jnp.where(i == 0, 0.0, pprev_ref[0][SUBLANES - 1:SUBLANES, :])
    row = lax.broadcasted_iota(I32, p.shape, 0)
    p_prev = jnp.where(row == 0, prev_last, pltpu.roll(p, 1, 0))
    xm = p + (p_prev - p) * mu_ref[...]
    r = xm[:, 0:WIDTH]
    k = xm[:, WIDTH:2 * WIDTH]
    v = xm[:, 2 * WIDTH:3 * WIDTH]
    wl = xm[:, 3 * WIDTH:3 * WIDTH + 128]
    al = xm[:, 3 * WIDTH + 128:3 * WIDTH + 256]
    gl = xm[:, 3 * WIDTH + 256:3 * WIDTH + 512]
    w0, a0, k_k, k_a = vec_ref[0:1, :], vec_ref[1:2, :], vec_ref[2:3, :], vec_ref[3:4, :]
    r_k, lnx_w, lnx_b = vec_ref[4:5, :], vec_ref[5:6, :], vec_ref[6:7, :]

    z = -(w0 + _mmh(jnp.tanh(wl), w2_ref[...]))
    softplus = jnp.maximum(z, 0.0) + jnp.log(1.0 + jnp.exp(-jnp.abs(z)))
    lw = -jnp.exp(-softplus - 0.5)
    a = _sigmoid(a0 + _mmh(al, a2_ref[...]))
    g = _mmh(_sigmoid(gl), g2_ref[...])
    hones = _head_ones(WIDTH)
    kk = k * k_k
    kk = kk / jnp.maximum(jnp.sqrt(_mmh(kk * kk, hones)), 1e-12)
    k2 = k * (1.0 + (a - 1.0) * k_a)
    a_s = -kk
    b_s = kk * a

    li = lax.broadcasted_iota(I32, (L, L), 0)
    lj = lax.broadcasted_iota(I32, (L, L), 1)
    tril_incl = li >= lj
    tril_strict = li > lj
    eye = (li == lj).astype(F32)
    csum = tril_incl.astype(F32)

    outs = []
    for c in range(tile // L):
        sl = slice(c * L, (c + 1) * L)
        lw_c = lw[sl]
        cum = _mmh(csum, lw_c)
        ref = cum[L // 2 - 1:L // 2, :]
        g_in = jnp.exp(cum - ref)
        g_ex = jnp.exp(cum - lw_c - ref)
        g_inv = jnp.exp(ref - cum)
        g_end = jnp.exp(cum[L - 1:L, :] - ref)
        g_tot = jnp.exp(cum[L - 1:L, :])
        g_ref = jnp.exp(ref)
        at = a_s[sl] * g_ex
        rt = r[sl] * g_in
        bt = b_s[sl] * g_inv
        kt = k2[sl] * g_inv
        bbar = bt * g_end
        kbar = kt * g_end
        v_c = v[sl]
        head_outs = []
        for h in range(N_HEADS):
            hs = slice(h * HEAD_DIM, (h + 1) * HEAD_DIM)
            a_h, r_h, b_h, k_h, v_h = at[:, hs], rt[:, hs], bt[:, hs], kt[:, hs], v_c[:, hs]
            sc = _mmh_nt(jnp.concatenate([a_h, r_h], axis=0), jnp.concatenate([b_h, k_h], axis=0))
            m_ab = jnp.where(tril_strict, sc[0:L, 0:L], 0.0)
            m_ak = jnp.where(tril_strict, sc[0:L, L:2 * L], 0.0)
            g_rb = jnp.where(tril_incl, sc[L:2 * L, 0:L], 0.0)
            g_rk = jnp.where(tril_incl, sc[L:2 * L, L:2 * L], 0.0)
            t_inv = eye + m_ab
            pw = m_ab
            for _ in range(5):
                pw = _mmh(pw, pw)
                t_inv = t_inv + _mmh(t_inv, pw)
            tw = _mmh(t_inv, jnp.concatenate([a_h, _mmh(m_ak, v_h)], axis=1))
            gw = _mmh(g_rb, tw)
            a_hat, u_hat = tw[:, 0:L] * g_ref[:, hs], tw[:, L:2 * L]
            r_hat = (r_h + gw[:, 0:L]) * g_ref[:, hs]
            o_hat = gw[:, L:2 * L] + _mmh(g_rk, v_h)
            phi_t = eye * g_tot[:, hs] + _mmh_tn(bbar[:, hs], a_hat)
            psi_t = _mmh_tn(bbar[:, hs], u_hat) + _mmh_tn(kbar[:, hs], v_h)
            st = st_ref[h]
            head_outs.append(_mmh(r_hat, st) + o_hat)
            st_ref[h] = _mmh(phi_t, st) + psi_t
        outs.append(jnp.concatenate(head_outs, axis=1))
    out = jnp.concatenate(outs, axis=0)

    inv_n = 1.0 / HEAD_DIM
    mean = _mmh(out, hones) * inv_n
    cen = out - mean
    var = _mmh(cen * cen, hones) * inv_n
    out = cen * lax.rsqrt(var + LNX_EPS) * lnx_w + lnx_b
    bonus = _mmh(r * k2 * r_k, hones) * v
    y_ref[0] = (out + bonus) * g


def _mmh_nt(a, b):
    return lax.dot_general(a, b, (((1,), (1,)), ((), ())), preferred_element_type=F32,
                           precision=lax.Precision.HIGHEST)


def _mmh_tn(a, b):
    return lax.dot_general(a, b, (((0,), (0,)), ((), ())), preferred_element_type=F32,
                           precision=lax.Precision.HIGHEST)


def _rwkv_call(p, mu, vecs, w2, a2, g2):
    bsz, s, pc = p.shape
    tile = RWKV_TILE
    const = lambda b, i: (0, 0)
    return pl.pallas_call(
        _rwkv_kernel,
        grid=(bsz, s // tile),
        in_specs=[pl.BlockSpec((1, tile, pc), lambda b, i: (b, i, 0)),
                  pl.BlockSpec((1, SUBLANES, pc),
                               lambda b, i: (b, jnp.maximum(i * (tile // SUBLANES) - 1, 0), 0)),
                  pl.BlockSpec((1, pc), const),
                  pl.BlockSpec((SUBLANES, WIDTH), const),
                  pl.BlockSpec((128, WIDTH), const),
                  pl.BlockSpec((128, WIDTH), const),
                  pl.BlockSpec((256, WIDTH), const)],
        out_specs=pl.BlockSpec((1, tile, WIDTH), lambda b, i: (b, i, 0)),
        out_shape=jax.ShapeDtypeStruct((bsz, s, WIDTH), F32),
        scratch_shapes=[pltpu.VMEM((N_HEADS, HEAD_DIM, HEAD_DIM), F32)],
        compiler_params=pltpu.CompilerParams(dimension_semantics=("arbitrary", "arbitrary"),
                                             vmem_limit_bytes=VMEM_LIMIT_BYTES),
        name="rwkv",
    )(p, p, mu, vecs, w2, a2, g2)


def _pad_rows(w, n):
    return jnp.pad(w, ((0, n - w.shape[0]), (0, 0)))


def _rwkv_params(mu_rwkv, w0, w2_decay, a0, a2_iclr, g2_gate, k_k, k_a, r_k, lnx_w, lnx_b):
    mu = jnp.concatenate([mu_rwkv[0:1536], _pad1(mu_rwkv[1536:1600], 128),
                          _pad1(mu_rwkv[1600:1664], 128), _pad1(mu_rwkv[1664:1824], 256)])
    vecs = jnp.stack([w0, a0, k_k, k_a, r_k.reshape(-1), lnx_w, lnx_b, jnp.zeros_like(w0)])
    return (mu.reshape(1, RWKV_PCOLS), vecs, _pad_rows(w2_decay, 128), _pad_rows(a2_iclr, 128),
            _pad_rows(g2_gate, 256))


def _pad1(v, n):
    return jnp.pad(v, (0, n - v.shape[0]))


def _pad_last(x, n):
    return jnp.pad(x, [(0, 0)] * (x.ndim - 1) + [(0, n - x.shape[-1])])


def _pad_cols(w, n):
    return _pad_last(w, n)


def _rwkv_weight_cols(w):
    return jnp.concatenate([w[:, 0:1536], _pad_cols(w[:, 1536:1600], 128),
                            _pad_cols(w[:, 1600:1664], 128), _pad_cols(w[:, 1664:1824], 256)], axis=1)


def _modulated_norm(x, gain, scale, shift):
    y = x * lax.rsqrt(jnp.mean(x * x, axis=-1, keepdims=True) + NORM_EPS)
    return y * gain * (1.0 + scale) + shift


def _proj_kernel(x_ref, g_ref, sc_ref, sh_ref, wr_ref, wq_ref, wk_ref, wv_ref, wg_ref,
                 pr_ref, q_ref, k_ref, v_ref, gate_ref):
    h = _modulated_norm(x_ref[0], g_ref[...], sc_ref[0], sh_ref[0]).astype(BF16)
    pr_ref[0] = jnp.dot(h, wr_ref[...], preferred_element_type=F32)
    q_ref[0] = jnp.dot(h, wq_ref[...], preferred_element_type=F32)
    k_ref[0] = jnp.dot(h, wk_ref[...], preferred_element_type=F32)
    v_ref[0] = jnp.dot(h, wv_ref[...], preferred_element_type=F32)
    gate_ref[0] = _sigmoid(jnp.dot(h, wg_ref[...], preferred_element_type=F32))


def _proj_call(x, norm_g, scale, shift, w_r, w_q, w_k, w_v, w_g):
    bsz, s, d = x.shape
    tm = 512
    tok = lambda b, i: (b, i, 0)
    per_b = lambda b, i: (b, 0, 0)
    const = lambda b, i: (0, 0)
    outs = [(RWKV_PCOLS, F32), (WIDTH, F32), (WIDTH, F32), (WIDTH, F32), (2 * d, F32)]
    return pl.pallas_call(
        _proj_kernel,
        grid=(bsz, s // tm),
        in_specs=[pl.BlockSpec((1, tm, d), tok),
                  pl.BlockSpec((1, d), const),
                  pl.BlockSpec((1, 1, d), per_b),
                  pl.BlockSpec((1, 1, d), per_b)]
                 + [pl.BlockSpec(w.shape, const) for w in (w_r, w_q, w_k, w_v, w_g)],
        out_specs=[pl.BlockSpec((1, tm, n), tok) for n, _ in outs],
        out_shape=[jax.ShapeDtypeStruct((bsz, s, n), dt) for n, dt in outs],
        compiler_params=pltpu.CompilerParams(dimension_semantics=("arbitrary", "arbitrary"),
                                             vmem_limit_bytes=VMEM_LIMIT_BYTES),
        name="proj",
    )(x, norm_g.reshape(1, d), scale, shift, w_r, w_q, w_k, w_v, w_g)


MOBA_PREP_TILE = 2 * MOBA_BLOCK


def _moba_prep_kernel(q_ref, k_ref, v_ref, qg_ref, kg_ref, qt_ref, kn_ref, vt_ref, sel_ref, kmean_ref):
    i = pl.program_id(1)
    tile = q_ref.shape[1]
    nb = kmean_ref.shape[0]
    blocks_per_tile = tile // MOBA_BLOCK

    @pl.when(i == 0)
    def _():
        kmean_ref[...] = jnp.zeros_like(kmean_ref)

    q, k, v = q_ref[0], k_ref[0], v_ref[0]
    hones = _head_ones(WIDTH)
    inv_n = 1.0 / HEAD_DIM
    qn = q * lax.rsqrt(_mmh(q * q, hones) * inv_n + NORM_EPS) * qg_ref[...] * (HEAD_DIM ** -0.5)
    kn = k * lax.rsqrt(_mmh(k * k, hones) * inv_n + NORM_EPS) * kg_ref[...]
    kn_ref[0] = kn.astype(BF16)

    kmean = kmean_ref[...]
    brow = lax.broadcasted_iota(I32, kmean.shape, 0)
    for blk in range(blocks_per_tile):
        km = jnp.mean(kn[blk * MOBA_BLOCK:(blk + 1) * MOBA_BLOCK], axis=0, keepdims=True)
        kmean = jnp.where(brow == i * blocks_per_tile + blk, km, kmean)
    kmean_ref[...] = kmean

    qnt = qn.T
    vt = v.T
    half = lax.broadcasted_iota(I32, (2 * HEAD_DIM, tile), 0) // HEAD_DIM
    n_iota = lax.broadcasted_iota(I32, (nb, tile), 0)
    q_blk = i * blocks_per_tile + lax.broadcasted_iota(I32, (nb, tile), 1) // MOBA_BLOCK
    for h in range(N_HEADS):
        pair = qnt[(h // 2) * 2 * HEAD_DIM:(h // 2 + 1) * 2 * HEAD_DIM]
        qt_ref[0, h] = jnp.where(half == h % 2, pair, 0.0).astype(BF16)
        for blk in range(blocks_per_tile):
            vt_ref[0, h, blk] = vt[h * HEAD_DIM:(h + 1) * HEAD_DIM,
                                   blk * MOBA_BLOCK:(blk + 1) * MOBA_BLOCK].astype(BF16)
        hs = slice(h * HEAD_DIM, (h + 1) * HEAD_DIM)
        gate = _mmh_nt(kmean[:, hs], qn[:, hs])
        work = jnp.where(n_iota < q_blk, gate, -jnp.inf)
        bits = jnp.zeros((1, tile), I32)
        for _ in range(MOBA_TOPK):
            m = jnp.max(work, axis=0, keepdims=True)
            idx = jnp.min(jnp.where(work == m, n_iota, nb), axis=0, keepdims=True)
            hit = n_iota == idx
            take = jnp.logical_and(hit, m > -jnp.inf)
            bits = bits | jnp.sum(jnp.where(take, jnp.left_shift(1, n_iota), 0), axis=0, keepdims=True)
            work = jnp.where(hit, -jnp.inf, work)
        sel_ref[0, h] = bits


def _moba_prep_call(q, k, v, q_norm_g, k_norm_g):
    bsz, s, w = q.shape
    tile = MOBA_PREP_TILE
    nb = s // MOBA_BLOCK
    assert nb <= 32, "block selection is a 32-bit mask"
    tok = lambda b, i: (b, i, 0)
    const = lambda b, i: (0, 0)
    return pl.pallas_call(
        _moba_prep_kernel,
        grid=(bsz, s // tile),
        in_specs=[pl.BlockSpec((1, tile, w), tok)] * 3 + [pl.BlockSpec((1, w), const)] * 2,
        out_specs=[pl.BlockSpec((1, N_HEADS, 2 * HEAD_DIM, tile), lambda b, i: (b, 0, 0, i)),
                   pl.BlockSpec((1, tile, w), tok),
                   pl.BlockSpec((1, N_HEADS, tile // MOBA_BLOCK, HEAD_DIM, MOBA_BLOCK),
                                lambda b, i: (b, 0, i, 0, 0)),
                   pl.BlockSpec((1, N_HEADS, 1, tile), lambda b, i: (b, 0, 0, i))],
        out_shape=[jax.ShapeDtypeStruct((bsz, N_HEADS, 2 * HEAD_DIM, s), BF16),
                   jax.ShapeDtypeStruct((bsz, s, w), BF16),
                   jax.ShapeDtypeStruct((bsz, N_HEADS, nb, HEAD_DIM, MOBA_BLOCK), BF16),
                   jax.ShapeDtypeStruct((bsz, N_HEADS, 1, s), I32)],
        scratch_shapes=[pltpu.VMEM((nb, w), F32)],
        compiler_params=pltpu.CompilerParams(dimension_semantics=("arbitrary", "arbitrary"),
                                             vmem_limit_bytes=VMEM_LIMIT_BYTES),
        name="moba_prep",
    )(q, k, v, jnp.tile(q_norm_g, N_HEADS).reshape(1, w), jnp.tile(k_norm_g, N_HEADS).reshape(1, w))


def _moba_attn_kernel(qt_ref, k_ref, vt_ref, sel_ref, o_ref):
    qb = pl.program_id(2)
    bl = MOBA_BLOCK
    qt = qt_ref[0, 0]
    sel = sel_ref[0, 0]

    def scores(j):
        kj = k_ref[0, pl.ds(pl.multiple_of(j * bl, bl), bl), :]
        return jnp.dot(kj, qt, preferred_element_type=F32)

    kpos = lax.broadcasted_iota(I32, (bl, bl), 0)
    qpos = lax.broadcasted_iota(I32, (bl, bl), 1)
    s = jnp.where(kpos <= qpos, scores(qb), NEG_INF)
    m = jnp.max(s, axis=0, keepdims=True)
    p = jnp.exp(s - m)
    l = jnp.sum(p, axis=0, keepdims=True)
    acc = jnp.dot(vt_ref[0, 0, qb], p.astype(BF16), preferred_element_type=F32)

    def body(j, carry):
        m, l, acc = carry
        picked = (jnp.right_shift(sel, j) & 1) != 0
        s = jnp.where(picked, scores(j), NEG_INF)
        m_new = jnp.maximum(m, jnp.max(s, axis=0, keepdims=True))
        alpha = jnp.exp(m - m_new)
        p = jnp.exp(s - m_new)
        l = alpha * l + jnp.sum(p, axis=0, keepdims=True)
        acc = alpha * acc + jnp.dot(vt_ref[0, 0, j], p.astype(BF16), preferred_element_type=F32)
        return m_new, l, acc

    m, l, acc = lax.fori_loop(0, qb, body, (m, l, acc))
    o_ref[0, 0] = acc / l


def _moba_attn_call(qt, kn, vt, sel):
    bsz, nh, _, s = qt.shape
    nb = s // MOBA_BLOCK
    return pl.pallas_call(
        _moba_attn_kernel,
        grid=(bsz, nh, nb),
        in_specs=[pl.BlockSpec((1, 1, 2 * HEAD_DIM, MOBA_BLOCK), lambda b, h, i: (b, h, 0, i)),
                  pl.BlockSpec((1, s, 2 * HEAD_DIM), lambda b, h, i: (b, 0, h // 2)),
                  pl.BlockSpec((1, 1, nb, HEAD_DIM, MOBA_BLOCK), lambda b, h, i: (b, h, 0, 0, 0)),
                  pl.BlockSpec((1, 1, 1, MOBA_BLOCK), lambda b, h, i: (b, h, 0, i))],
        out_specs=pl.BlockSpec((1, 1, HEAD_DIM, MOBA_BLOCK), lambda b, h, i: (b, h, 0, i)),
        out_shape=jax.ShapeDtypeStruct((bsz, nh, HEAD_DIM, s), F32),
        compiler_params=pltpu.CompilerParams(
            dimension_semantics=("arbitrary", "arbitrary", "arbitrary"),
            vmem_limit_bytes=VMEM_LIMIT_BYTES),
        name="moba_attn",
    )(qt, kn, vt, sel)


def _moba_branch(q, k, v, q_norm_g, k_norm_g):
    qt, kn, vt, sel = _moba_prep_call(q, k, v, q_norm_g, k_norm_g)
    yt = _moba_attn_call(qt, kn, vt, sel)
    return yt.reshape(yt.shape[0], WIDTH, yt.shape[3])


def _merge_kernel(x_ref, ya_ref, ybt_ref, gates_ref, wb0_ref, wb1_ref, wo_ref, g1_ref, n2_ref, sc2_ref,
                  sh2_ref, wq_ref, x1_ref, h2_ref, q_ref):
    d = x_ref.shape[2]
    ys_a = _mm(ya_ref[0], wb0_ref[...])
    ys_b = _mm_tn(ybt_ref[0], wb1_ref[...])
    gates = gates_ref[0]
    mixed = gates[:, 0:d] * ys_a + gates[:, d:2 * d] * ys_b
    x1 = x_ref[0] + g1_ref[0] * _mm(mixed, wo_ref[...])
    x1_ref[0] = x1
    h2 = _modulated_norm(x1, n2_ref[...], sc2_ref[0], sh2_ref[0])
    h2_ref[0] = h2
    q_ref[0] = _mm(h2, wq_ref[...])


def _merge_call(x, y_a, y_bt, gates, w_b0, w_b1, w_out, gate1, norm2_g, scale2, shift2, w_peer_q):
    bsz, s, d = x.shape
    tm = 256
    nq = w_peer_q.shape[1]
    tok = lambda b, i: (b, i, 0)
    per_b = lambda b, i: (b, 0, 0)
    const = lambda b, i: (0, 0)
    return pl.pallas_call(
        _merge_kernel,
        grid=(bsz, s // tm),
        in_specs=[pl.BlockSpec((1, tm, d), tok),
                  pl.BlockSpec((1, tm, WIDTH), tok),
                  pl.BlockSpec((1, WIDTH, tm), lambda b, i: (b, 0, i)),
                  pl.BlockSpec((1, tm, 2 * d), tok),
                  pl.BlockSpec((WIDTH, d), const),
                  pl.BlockSpec((WIDTH, d), const),
                  pl.BlockSpec((d, d), const),
                  pl.BlockSpec((1, 1, d), per_b),
                  pl.BlockSpec((1, d), const),
                  pl.BlockSpec((1, 1, d), per_b),
                  pl.BlockSpec((1, 1, d), per_b),
                  pl.BlockSpec((d, nq), const)],
        out_specs=[pl.BlockSpec((1, tm, d), tok), pl.BlockSpec((1, tm, d), tok),
                   pl.BlockSpec((1, tm, nq), tok)],
        out_shape=[jax.ShapeDtypeStruct((bsz, s, d), F32), jax.ShapeDtypeStruct((bsz, s, d), F32),
                   jax.ShapeDtypeStruct((bsz, s, nq), F32)],
        compiler_params=pltpu.CompilerParams(dimension_semantics=("arbitrary", "arbitrary"),
                                             vmem_limit_bytes=VMEM_LIMIT_BYTES),
        name="merge",
    )(x, y_a, y_bt, gates, w_b0, w_b1, w_out, gate1, norm2_g.reshape(1, d), scale2, shift2, w_peer_q)


def _topk_rows(x, k, payload=None):
    n = x.shape[0]
    iota = lax.broadcasted_iota(I32, x.shape, 0)
    vals, picks = [], []
    work = x
    for _ in range(k):
        m = jnp.max(work, axis=0, keepdims=True)
        idx = jnp.min(jnp.where(work == m, iota, n), axis=0, keepdims=True)
        hit = iota == idx
        vals.append(m)
        picks.append(idx if payload is None else jnp.max(jnp.where(hit, payload, -1), axis=0, keepdims=True))
        work = jnp.where(hit, -jnp.inf, work)
    return jnp.concatenate(vals, axis=0), jnp.concatenate(picks, axis=0)


def _route_kernel(q_ref, keys_ref, idx_ref, gw_ref):
    k = PEER_TOPK
    eids, gws = [], []
    for hp in range(PEER_HEADS):
        q1 = q_ref[:, (2 * hp) * PEER_KEYS:(2 * hp + 1) * PEER_KEYS]
        q2 = q_ref[:, (2 * hp + 1) * PEER_KEYS:(2 * hp + 2) * PEER_KEYS]
        s1 = _mm_nt(keys_ref[0, hp], q1)
        s2 = _mm_nt(keys_ref[1, hp], q2)
        v1, i1 = _topk_rows(s1, k)
        v2, i2 = _topk_rows(s2, k)
        comb = jnp.concatenate([v1[j:j + 1] + v2 for j in range(k)], axis=0)
        expert = jnp.concatenate([i1[j:j + 1] * PEER_KEYS + i2 for j in range(k)], axis=0)
        sc, eid = _topk_rows(comb, k, payload=expert)
        e = jnp.exp(sc - sc[0:1])
        gws.append(e / jnp.sum(e, axis=0, keepdims=True))
        eids.append(eid)
    idx_ref[...] = jnp.concatenate(eids, axis=0).T
    gw_ref[...] = jnp.concatenate(gws, axis=0).T


def _route_call(q, sub_keys):
    t, nq = q.shape
    tt = 256
    return pl.pallas_call(
        _route_kernel,
        grid=(t // tt,),
        in_specs=[pl.BlockSpec((tt, nq), lambda i: (i, 0)),
                  pl.BlockSpec(sub_keys.shape, lambda i: (0, 0, 0, 0))],
        out_specs=[pl.BlockSpec((tt, PEER_PICKS), lambda i: (i, 0)),
                   pl.BlockSpec((tt, PEER_PICKS), lambda i: (i, 0))],
        out_shape=[jax.ShapeDtypeStruct((t, PEER_PICKS), I32), jax.ShapeDtypeStruct((t, PEER_PICKS), F32)],
        compiler_params=pltpu.CompilerParams(dimension_semantics=("arbitrary",),
                                             vmem_limit_bytes=VMEM_LIMIT_BYTES),
        name="route",
    )(q, sub_keys)


PEER_TOKENS_PER_STEP = 64
_HI_MASK = 0xFFFF0000


def _pack_table(tab):
    n, d = tab.shape
    bits = lax.bitcast_convert_type(tab.astype(BF16), jnp.uint16).astype(jnp.uint32)
    packed = bits[:n // 2] | (bits[n // 2:] << 16)
    return packed.reshape(n // 2, d // LANES, LANES)


def _expert_row(tab_ref, e):
    half = tab_ref.shape[0]
    word = tab_ref[e & (half - 1)]
    shift = ((1 - e // half) * 16).astype(jnp.uint32)
    return lax.bitcast_convert_type(jnp.left_shift(word, shift) & jnp.uint32(_HI_MASK), F32)


def _gelu(x):
    return 0.5 * x * (1.0 + lax.erf(x * (2.0 ** -0.5)))


def _peer_act_kernel(idx_ref, h_ref, gw_ref, tab_ref, act_ref, prod_ref):
    ones = jnp.ones((SUBLANES, LANES), F32)

    def token(t, carry):
        h = h_ref[t]
        for s in range(PEER_PICKS):
            prod = _expert_row(tab_ref, idx_ref[t, s]) * h
            prod_ref[s:s + 1, :] = jnp.sum(prod, axis=0, keepdims=True)
        dots = _mmh_nt(ones, prod_ref[...])[0:1]
        act_ref[t] = _gelu(dots) * gw_ref[t]
        return carry

    lax.fori_loop(0, h_ref.shape[0], token, 0)


def _peer_act_call(idx, h2, gw, tab):
    t = idx.shape[0]
    tt = PEER_TOKENS_PER_STEP
    return pl.pallas_call(
        _peer_act_kernel,
        grid=(t // tt,),
        in_specs=[pl.BlockSpec((tt, PEER_PICKS), lambda i: (i, 0), memory_space=pltpu.SMEM),
                  pl.BlockSpec((tt, SUBLANES, LANES), lambda i: (i, 0, 0)),
                  pl.BlockSpec((tt, 1, PEER_PICKS), lambda i: (i, 0, 0)),
                  pl.BlockSpec(tab.shape, lambda i: (0, 0, 0), pipeline_mode=pl.Buffered(1))],
        out_specs=pl.BlockSpec((tt, 1, PEER_PICKS), lambda i: (i, 0, 0)),
        out_shape=jax.ShapeDtypeStruct((t, 1, PEER_PICKS), F32),
        scratch_shapes=[pltpu.VMEM((PEER_PICKS, LANES), F32)],
        compiler_params=pltpu.CompilerParams(dimension_semantics=("arbitrary",),
                                             vmem_limit_bytes=VMEM_LIMIT_BYTES),
        name="peer_act",
    )(idx, h2, gw.reshape(t, 1, PEER_PICKS), tab)


def _peer_out_kernel(idx_ref, act_ref, x_ref, g2_ref, tab_ref, o_ref):
    g2 = g2_ref[0]

    def token(t, carry):
        acc = jnp.zeros((SUBLANES, LANES), F32)
        for s in range(PEER_PICKS):
            acc = acc + act_ref[t, s] * _expert_row(tab_ref, idx_ref[t, s])
        o_ref[t] = x_ref[t] + g2 * acc
        return carry

    lax.fori_loop(0, x_ref.shape[0], token, 0)


def _peer_out_call(idx, act, x1, gate2, tab, tokens_per_batch):
    t = idx.shape[0]
    tt = PEER_TOKENS_PER_STEP
    steps_per_batch = tokens_per_batch // tt
    return pl.pallas_call(
        _peer_out_kernel,
        grid=(t // tt,),
        in_specs=[pl.BlockSpec((tt, PEER_PICKS), lambda i: (i, 0), memory_space=pltpu.SMEM),
                  pl.BlockSpec((tt, PEER_PICKS), lambda i: (i, 0), memory_space=pltpu.SMEM),
                  pl.BlockSpec((tt, SUBLANES, LANES), lambda i: (i, 0, 0)),
                  pl.BlockSpec((1, SUBLANES, LANES), lambda i: (i // steps_per_batch, 0, 0)),
                  pl.BlockSpec(tab.shape, lambda i: (0, 0, 0), pipeline_mode=pl.Buffered(1))],
        out_specs=pl.BlockSpec((tt, SUBLANES, LANES), lambda i: (i, 0, 0)),
        out_shape=jax.ShapeDtypeStruct((t, SUBLANES, LANES), F32),
        compiler_params=pltpu.CompilerParams(dimension_semantics=("arbitrary",),
                                             vmem_limit_bytes=VMEM_LIMIT_BYTES),
        name="peer_out",
    )(idx, act, x1, gate2, tab)


def _peer_ffn_residual(x1, h2, q, gate2, sub_keys, peer_u, peer_v):
    bsz, s, d = x1.shape
    t = bsz * s
    idx, gw = _route_call(q.reshape(t, -1), sub_keys.astype(BF16))
    act = _peer_act_call(idx, h2.reshape(t, SUBLANES, LANES), gw, _pack_table(peer_u))
    out = _peer_out_call(idx, act.reshape(t, PEER_PICKS), x1.reshape(t, SUBLANES, LANES),
                         gate2.reshape(bsz, SUBLANES, LANES), _pack_table(peer_v), s)
    return out.reshape(bsz, s, d)


def kernel(x, c, w_ada, b_ada, norm1_g, w_in, mu_rwkv, w0, w2_decay, a0, a2_iclr, g2_gate, k_k, k_a,
           r_k, lnx_w, lnx_b, q_norm_g, k_norm_g, w_branch, w_out, norm2_g, w_peer_q, peer_sub_keys,
           peer_u, peer_v):
    bsz, s, d = x.shape
    ada = _ada_call(c, w_ada, b_ada).reshape(bsz, 6, 1, d)
    shift1, scale1, gate1, shift2, scale2, gate2 = (ada[:, j] for j in range(6))
    w_r = _rwkv_weight_cols(w_in[:, 0:1824]).astype(BF16)
    w_q, w_k, w_v = (w_in[:, 1824 + j * WIDTH:1824 + (j + 1) * WIDTH].astype(BF16) for j in range(3))
    w_g = w_in[:, 1824 + 3 * WIDTH:].astype(BF16)
    p_r, q, k, v, gates = _proj_call(x, norm1_g, scale1, shift1, w_r, w_q, w_k, w_v, w_g)
    y_a = _rwkv_call(p_r, *_rwkv_params(mu_rwkv, w0, w2_decay, a0, a2_iclr, g2_gate, k_k, k_a, r_k,
                                        lnx_w, lnx_b))
    y_bt = _moba_branch(q, k, v, q_norm_g, k_norm_g)
    x1, h2, pq = _merge_call(x, y_a, y_bt, gates, w_branch[0].astype(BF16), w_branch[1].astype(BF16),
                             w_out.astype(BF16), gate1, norm2_g, scale2, shift2, w_peer_q.astype(BF16))
    return _peer_ffn_residual(x1, h2, pq, gate2, peer_sub_keys, peer_u, peer_v)
```

```python
import functools

import jax
import jax.numpy as jnp
from jax import lax
from jax.experimental import pallas as pl
from jax.experimental.pallas import tpu as pltpu

F32 = jnp.float32
BF16 = jnp.bfloat16
I32 = jnp.int32

LANES = 128
SUBLANES = 8
VMEM_LIMIT_BYTES = 56 * 1024 * 1024

NORM_EPS = 1e-6
LNX_EPS = 64e-5
NEG_INF = -1e30

HEAD_DIM = 64
N_HEADS = 8
WIDTH = N_HEADS * HEAD_DIM
RWKV_CHUNK = 64
RWKV_TILE = 128
RWKV_PCOLS = 2048
MOBA_BLOCK = 256
MOBA_TOPK = 3
PEER_HEADS = 8
PEER_KEYS = 128
PEER_TOPK = 16
PEER_PICKS = PEER_HEADS * PEER_TOPK


def _mm(a, b):
    return jnp.dot(a.astype(BF16), b.astype(BF16), preferred_element_type=F32)


def _mm_nt(a, b):
    return lax.dot_general(a.astype(BF16), b.astype(BF16), (((1,), (1,)), ((), ())),
                           preferred_element_type=F32)


def _mm_tn(a, b):
    return lax.dot_general(a.astype(BF16), b.astype(BF16), (((0,), (0,)), ((), ())),
                           preferred_element_type=F32)


def _mm_split(a, b):
    hi = a.astype(BF16)
    lo = (a - hi.astype(F32)).astype(BF16)
    b = b.astype(BF16)
    return (jnp.dot(hi, b, preferred_element_type=F32) + jnp.dot(lo, b, preferred_element_type=F32))


def _mmh(a, b):
    return jnp.dot(a, b, preferred_element_type=F32, precision=lax.Precision.HIGHEST)


def _head_ones(n):
    r = lax.broadcasted_iota(I32, (n, n), 0) // HEAD_DIM
    c = lax.broadcasted_iota(I32, (n, n), 1) // HEAD_DIM
    return (r == c).astype(F32)


def _sigmoid(x):
    return 1.0 / (1.0 + jnp.exp(-x))


def _ada_kernel(c_ref, w_ref, b_ref, o_ref):
    c = c_ref[...]
    o_ref[...] = _mmh(c * _sigmoid(c), w_ref[...]) + b_ref[...]


def _ada_call(c, w_ada, b_ada):
    bsz, d = c.shape
    n = w_ada.shape[1]
    tn = 1536
    return pl.pallas_call(
        _ada_kernel,
        grid=(n // tn,),
        in_specs=[pl.BlockSpec((bsz, d), lambda j: (0, 0)),
                  pl.BlockSpec((d, tn), lambda j: (0, j)),
                  pl.BlockSpec((1, tn), lambda j: (0, j))],
        out_specs=pl.BlockSpec((bsz, tn), lambda j: (0, j)),
        out_shape=jax.ShapeDtypeStruct((bsz, n), F32),
        compiler_params=pltpu.CompilerParams(dimension_semantics=("arbitrary",),
                                             vmem_limit_bytes=VMEM_LIMIT_BYTES),
        name="ada",
    )(c, w_ada, b_ada.reshape(1, n))


def _rwkv_kernel(p_ref, pprev_ref, mu_ref, vec_ref, w2_ref, a2_ref, g2_ref, y_ref, st_ref):
    i = pl.program_id(1)
    tile = p_ref.shape[1]
    L = RWKV_CHUNK

    @pl.when(i == 0)
    def _():
        st_ref[...] = jnp.zeros_like(st_ref)

    p = p_ref[0]
    prev_last = jnp.where(i == 0, 0.0, pprev_ref[0][SUBLANES - 1:SUBLANES, :])
    row = lax.broadcasted_iota(I32, p.shape, 0)
    p_prev = jnp.where(row == 0, prev_last, pltpu.roll(p, 1, 0))
    xm = p + (p_prev - p) * mu_ref[...]
    r = xm[:, 0:WIDTH]
    k = xm[:, WIDTH:2 * WIDTH]
    v = xm[:, 2 * WIDTH:3 * WIDTH]
    wl = xm[:, 3 * WIDTH:3 * WIDTH + 128]
    al = xm[:, 3 * WIDTH + 128:3 * WIDTH + 256]
    gl = xm[:, 3 * WIDTH + 256:3 * WIDTH + 512]
    w0, a0, k_k, k_a = vec_ref[0:1, :], vec_ref[1:2, :], vec_ref[2:3, :], vec_ref[3:4, :]
    r_k, lnx_w, lnx_b = vec_ref[4:5, :], vec_ref[5:6, :], vec_ref[6:7, :]

    z = -(w0 + _mmh(jnp.tanh(wl), w2_ref[...]))
    softplus = jnp.maximum(z, 0.0) + jnp.log(1.0 + jnp.exp(-jnp.abs(z)))
    lw = -jnp.exp(-softplus - 0.5)
    a = _sigmoid(a0 + _mmh(al, a2_ref[...]))
    g = _mmh(_sigmoid(gl), g2_ref[...])
    hones = _head_ones(WIDTH)
    kk = k * k_k
    kk = kk / jnp.maximum(jnp.sqrt(_mm(kk * kk, hones)), 1e-12)
    k2 = k * (1.0 + (a - 1.0) * k_a)
    a_s = -kk
    b_s = kk * a

    li = lax.broadcasted_iota(I32, (L, L), 0)
    lj = lax.broadcasted_iota(I32, (L, L), 1)
    tril_incl = li >= lj
    tril_strict = li > lj
    eye = (li == lj).astype(F32)
    csum = tril_incl.astype(F32)

    n_chunks = tile // L
    items = [(c, h) for c in range(n_chunks) for h in range(N_HEADS)]
    zeros_ll = jnp.zeros((L, L), F32)
    per_chunk = []
    for c in range(n_chunks):
        sl = slice(c * L, (c + 1) * L)
        lw_c = lw[sl]
        cum = _mmh(csum, lw_c)
        ref = cum[L // 2 - 1:L // 2, :]
        g_inv = jnp.exp(ref - cum)
        g_end = jnp.exp(cum[L - 1:L, :] - ref)
        bt = b_s[sl] * g_inv
        kt = k2[sl] * g_inv
        per_chunk.append(dict(
            at=a_s[sl] * jnp.exp(cum - lw_c - ref), rt=r[sl] * jnp.exp(cum - ref), bt=bt, kt=kt,
            bbar=bt * g_end, kbar=kt * g_end, v=v[sl], g_tot=jnp.exp(cum[L - 1:L, :]), g_ref=jnp.exp(ref)))

    def head(c, h, name):
        return per_chunk[c][name][:, h * HEAD_DIM:(h + 1) * HEAD_DIM]

    sc = [_mm_nt(jnp.concatenate([head(c, h, "at"), head(c, h, "rt")], axis=0),
                 jnp.concatenate([head(c, h, "bt"), head(c, h, "kt")], axis=0)) for c, h in items]
    m_ab = [jnp.where(tril_strict, s[0:L, 0:L], 0.0) for s in sc]
    m_ak = [jnp.where(tril_strict, s[0:L, L:2 * L], 0.0) for s in sc]
    tril_incl2 = (lax.broadcasted_iota(I32, (L, 2 * L), 0)
                  >= lax.broadcasted_iota(I32, (L, 2 * L), 1) % L)
    g_rbk = [jnp.where(tril_incl2, s[L:2 * L, :], 0.0) for s in sc]
    akv = [_mm(m, head(c, h, "v")) for m, (c, h) in zip(m_ak, items)]
    pw = [_mm(m, m) for m in m_ab]
    t_inv = [eye + m for m in m_ab]
    for _ in range(4):
        both = [_mm(jnp.concatenate([p, t], axis=0), p) for p, t in zip(pw, t_inv)]
        t_inv = [t + b[L:2 * L] for t, b in zip(t_inv, both)]
        pw = [b[0:L] for b in both]
    t_inv = [t + _mm(t, p) for t, p in zip(t_inv, pw)]
    tw = [_mm(t, jnp.concatenate([head(c, h, "at"), x], axis=1)) for t, x, (c, h) in zip(t_inv, akv, items)]
    ro = [_mm(g, jnp.concatenate([w, jnp.concatenate([zeros_ll, head(c, h, "v")], axis=1)], axis=0))
          for g, w, (c, h) in zip(g_rbk, tw, items)]
    a_hat = [w[:, 0:L] * head(c, h, "g_ref") for w, (c, h) in zip(tw, items)]
    r_hat = [(head(c, h, "rt") + x[:, 0:L]) * head(c, h, "g_ref") for x, (c, h) in zip(ro, items)]
    pp = [_mm_tn(jnp.concatenate([head(c, h, "bbar"), head(c, h, "kbar")], axis=0),
                 jnp.concatenate([jnp.concatenate([a, w[:, L:2 * L]], axis=1),
                                  jnp.concatenate([zeros_ll, head(c, h, "v")], axis=1)], axis=0))
          for a, w, (c, h) in zip(a_hat, tw, items)]
    state = [st_ref[h] for h in range(N_HEADS)]
    outs = []
    for c in range(n_chunks):
        head_outs = []
        for h in range(N_HEADS):
            i = c * N_HEADS + h
            phi_t = eye * head(c, h, "g_tot") + pp[i][:, 0:L]
            both = _mm(jnp.concatenate([r_hat[i], phi_t], axis=0), state[h])
            head_outs.append(both[0:L] + ro[i][:, L:2 * L])
            state[h] = both[L:2 * L] + pp[i][:, L:2 * L]
        outs.append(jnp.concatenate(head_outs, axis=1))
    for h in range(N_HEADS):
        st_ref[h] = state[h]
    out = jnp.concatenate(outs, axis=0)

    inv_n = 1.0 / HEAD_DIM
    mean = _mm(out, hones) * inv_n
    cen = out - mean
    var = _mm(cen * cen, hones) * inv_n
    out = cen * lax.rsqrt(var + LNX_EPS) * lnx_w + lnx_b
    bonus = _mm(r * k2 * r_k, hones) * v
    y_ref[0] = (out + bonus) * g


def _mmh_nt(a, b):
    return lax.dot_general(a, b, (((1,), (1,)), ((), ())), preferred_element_type=F32,
                           precision=lax.Precision.HIGHEST)


def _mmh_tn(a, b):
    return lax.dot_general(a, b, (((0,), (0,)), ((), ())), preferred_element_type=F32,
                           precision=lax.Precision.HIGHEST)


def _rwkv_call(p, mu, vecs, w2, a2, g2):
    bsz, s, pc = p.shape
    tile = RWKV_TILE
    const = lambda b, i: (0, 0)
    return pl.pallas_call(
        _rwkv_kernel,
        grid=(bsz, s // tile),
        in_specs=[pl.BlockSpec((1, tile, pc), lambda b, i: (b, i, 0)),
                  pl.BlockSpec((1, SUBLANES, pc),
                               lambda b, i: (b, jnp.maximum(i * (tile // SUBLANES) - 1, 0), 0)),
                  pl.BlockSpec((1, pc), const),
                  pl.BlockSpec((SUBLANES, WIDTH), const),
                  pl.BlockSpec((128, WIDTH), const),
                  pl.BlockSpec((128, WIDTH), const),
                  pl.BlockSpec((256, WIDTH), const)],
        out_specs=pl.BlockSpec((1, tile, WIDTH), lambda b, i: (b, i, 0)),
        out_shape=jax.ShapeDtypeStruct((bsz, s, WIDTH), F32),
        scratch_shapes=[pltpu.VMEM((N_HEADS, HEAD_DIM, HEAD_DIM), F32)],
        compiler_params=pltpu.CompilerParams(dimension_semantics=("arbitrary", "arbitrary"),
                                             vmem_limit_bytes=VMEM_LIMIT_BYTES),
        name="rwkv",
    )(p, p, mu, vecs, w2, a2, g2)


def _pad_rows(w, n):
    return jnp.pad(w, ((0, n - w.shape[0]), (0, 0)))


def _rwkv_params(mu_rwkv, w0, w2_decay, a0, a2_iclr, g2_gate, k_k, k_a, r_k, lnx_w, lnx_b):
    mu = jnp.concatenate([mu_rwkv[0:1536], _pad1(mu_rwkv[1536:1600], 128),
                          _pad1(mu_rwkv[1600:1664], 128), _pad1(mu_rwkv[1664:1824], 256)])
    vecs = jnp.stack([w0, a0, k_k, k_a, r_k.reshape(-1), lnx_w, lnx_b, jnp.zeros_like(w0)])
    return (mu.reshape(1, RWKV_PCOLS), vecs, _pad_rows(w2_decay, 128), _pad_rows(a2_iclr, 128),
            _pad_rows(g2_gate, 256))


def _pad1(v, n):
    return jnp.pad(v, (0, n - v.shape[0]))


def _pad_last(x, n):
    return jnp.pad(x, [(0, 0)] * (x.ndim - 1) + [(0, n - x.shape[-1])])


def _pad_cols(w, n):
    return _pad_last(w, n)


def _rwkv_weight_cols(w):
    return jnp.concatenate([w[:, 0:1536], _pad_cols(w[:, 1536:1600], 128),
                            _pad_cols(w[:, 1600:1664], 128), _pad_cols(w[:, 1664:1824], 256)], axis=1)


def _modulated_norm(x, gain, scale, shift):
    y = x * lax.rsqrt(jnp.mean(x * x, axis=-1, keepdims=True) + NORM_EPS)
    return y * gain * (1.0 + scale) + shift


def _proj_kernel(x_ref, g_ref, sc_ref, sh_ref, wr_ref, wq_ref, wk_ref, wv_ref, wg_ref,
                 pr_ref, q_ref, k_ref, v_ref, gate_ref):
    h = _modulated_norm(x_ref[0], g_ref[...], sc_ref[0], sh_ref[0]).astype(BF16)
    pr_ref[0] = jnp.dot(h, wr_ref[...], preferred_element_type=F32)
    q_ref[0] = jnp.dot(h, wq_ref[...], preferred_element_type=F32)
    k_ref[0] = jnp.dot(h, wk_ref[...], preferred_element_type=F32)
    v_ref[0] = jnp.dot(h, wv_ref[...], preferred_element_type=F32)
    gate_ref[0] = _sigmoid(jnp.dot(h, wg_ref[...], preferred_element_type=F32))


def _proj_call(x, norm_g, scale, shift, w_r, w_q, w_k, w_v, w_g):
    bsz, s, d = x.shape
    tm = 512
    tok = lambda b, i: (b, i, 0)
    per_b = lambda b, i: (b, 0, 0)
    const = lambda b, i: (0, 0)
    outs = [(RWKV_PCOLS, F32), (WIDTH, F32), (WIDTH, F32), (WIDTH, F32), (2 * d, F32)]
    return pl.pallas_call(
        _proj_kernel,
        grid=(bsz, s // tm),
        in_specs=[pl.BlockSpec((1, tm, d), tok),
                  pl.BlockSpec((1, d), const),
                  pl.BlockSpec((1, 1, d), per_b),
                  pl.BlockSpec((1, 1, d), per_b)]
                 + [pl.BlockSpec(w.shape, const) for w in (w_r, w_q, w_k, w_v, w_g)],
        out_specs=[pl.BlockSpec((1, tm, n), tok) for n, _ in outs],
        out_shape=[jax.ShapeDtypeStruct((bsz, s, n), dt) for n, dt in outs],
        compiler_params=pltpu.CompilerParams(dimension_semantics=("arbitrary", "arbitrary"),
                                             vmem_limit_bytes=VMEM_LIMIT_BYTES),
        name="proj",
    )(x, norm_g.reshape(1, d), scale, shift, w_r, w_q, w_k, w_v, w_g)


MOBA_PREP_TILE = 2 * MOBA_BLOCK


def _moba_prep_kernel(q_ref, k_ref, v_ref, qg_ref, kg_ref, qt_ref, kn_ref, vt_ref, sel_ref, kmean_ref):
    i = pl.program_id(1)
    tile = q_ref.shape[1]
    nb = kmean_ref.shape[0]
    blocks_per_tile = tile // MOBA_BLOCK

    @pl.when(i == 0)
    def _():
        kmean_ref[...] = jnp.zeros_like(kmean_ref)

    q, k, v = q_ref[0], k_ref[0], v_ref[0]
    hones = _head_ones(WIDTH)
    inv_n = 1.0 / HEAD_DIM
    qn = q * lax.rsqrt(_mm_split(q * q, hones) * inv_n + NORM_EPS) * qg_ref[...] * (HEAD_DIM ** -0.5)
    kn = k * lax.rsqrt(_mm_split(k * k, hones) * inv_n + NORM_EPS) * kg_ref[...]
    kn_ref[0] = kn.astype(BF16)

    kmean = kmean_ref[...]
    brow = lax.broadcasted_iota(I32, kmean.shape, 0)
    for blk in range(blocks_per_tile):
        km = jnp.mean(kn[blk * MOBA_BLOCK:(blk + 1) * MOBA_BLOCK], axis=0, keepdims=True)
        kmean = jnp.where(brow == i * blocks_per_tile + blk, km, kmean)
    kmean_ref[...] = kmean

    qnt = qn.T
    vt = v.T
    half = lax.broadcasted_iota(I32, (2 * HEAD_DIM, tile), 0) // HEAD_DIM
    n_iota = lax.broadcasted_iota(I32, (nb, tile), 0)
    q_blk = i * blocks_per_tile + lax.broadcasted_iota(I32, (nb, tile), 1) // MOBA_BLOCK
    for h in range(N_HEADS):
        pair = qnt[(h // 2) * 2 * HEAD_DIM:(h // 2 + 1) * 2 * HEAD_DIM]
        qt_ref[0, h] = jnp.where(half == h % 2, pair, 0.0).astype(BF16)
        for blk in range(blocks_per_tile):
            vt_ref[0, h, blk] = vt[h * HEAD_DIM:(h + 1) * HEAD_DIM,
                                   blk * MOBA_BLOCK:(blk + 1) * MOBA_BLOCK].astype(BF16)
        hs = slice(h * HEAD_DIM, (h + 1) * HEAD_DIM)
        gate = _mmh_nt(kmean[:, hs], qn[:, hs])
        work = jnp.where(n_iota < q_blk, gate, -jnp.inf)
        bits = jnp.zeros((1, tile), I32)
        for _ in range(MOBA_TOPK):
            m = jnp.max(work, axis=0, keepdims=True)
            idx = jnp.min(jnp.where(work == m, n_iota, nb), axis=0, keepdims=True)
            hit = n_iota == idx
            take = jnp.logical_and(hit, m > -jnp.inf)
            bits = bits | jnp.sum(jnp.where(take, jnp.left_shift(1, n_iota), 0), axis=0, keepdims=True)
            work = jnp.where(hit, -jnp.inf, work)
        sel_ref[0, h] = bits


def _moba_prep_call(q, k, v, q_norm_g, k_norm_g):
    bsz, s, w = q.shape
    tile = MOBA_PREP_TILE
    nb = s // MOBA_BLOCK
    assert nb <= 32, "block selection is a 32-bit mask"
    tok = lambda b, i: (b, i, 0)
    const = lambda b, i: (0, 0)
    return pl.pallas_call(
        _moba_prep_kernel,
        grid=(bsz, s // tile),
        in_specs=[pl.BlockSpec((1, tile, w), tok)] * 3 + [pl.BlockSpec((1, w), const)] * 2,
        out_specs=[pl.BlockSpec((1, N_HEADS, 2 * HEAD_DIM, tile), lambda b, i: (b, 0, 0, i)),
                   pl.BlockSpec((1, tile, w), tok),
                   pl.BlockSpec((1, N_HEADS, tile // MOBA_BLOCK, HEAD_DIM, MOBA_BLOCK),
                                lambda b, i: (b, 0, i, 0, 0)),
                   pl.BlockSpec((1, N_HEADS, 1, tile), lambda b, i: (b, 0, 0, i))],
        out_shape=[jax.ShapeDtypeStruct((bsz, N_HEADS, 2 * HEAD_DIM, s), BF16),
                   jax.ShapeDtypeStruct((bsz, s, w), BF16),
                   jax.ShapeDtypeStruct((bsz, N_HEADS, nb, HEAD_DIM, MOBA_BLOCK), BF16),
                   jax.ShapeDtypeStruct((bsz, N_HEADS, 1, s), I32)],
        scratch_shapes=[pltpu.VMEM((nb, w), F32)],
        compiler_params=pltpu.CompilerParams(dimension_semantics=("arbitrary", "arbitrary"),
                                             vmem_limit_bytes=VMEM_LIMIT_BYTES),
        name="moba_prep",
    )(q, k, v, jnp.tile(q_norm_g, N_HEADS).reshape(1, w), jnp.tile(k_norm_g, N_HEADS).reshape(1, w))


MOBA_KV_UNROLL = 2


def _moba_attn_kernel(qt_ref, k_ref, vt_ref, sel_ref, o_ref):
    qb = pl.program_id(2)
    bl = MOBA_BLOCK
    nb = vt_ref.shape[2]
    kpos = lax.broadcasted_iota(I32, (bl, bl), 0)
    qpos = lax.broadcasted_iota(I32, (bl, bl), 1)

    def key_block(j):
        return k_ref[0, pl.ds(pl.multiple_of(j * bl, bl), bl), :]

    k_own = key_block(qb)
    state = []
    for e in range(2):
        s = jnp.where(kpos <= qpos, jnp.dot(k_own, qt_ref[0, e], preferred_element_type=F32), NEG_INF)
        m = jnp.max(s, axis=0, keepdims=True)
        p = jnp.exp(s - m)
        l = jnp.sum(p, axis=0, keepdims=True)
        acc = jnp.dot(vt_ref[0, e, qb], p.astype(BF16), preferred_element_type=F32)
        state += [m, l, acc]

    def body(it, carry):
        carry = list(carry)
        js = [it * MOBA_KV_UNROLL + u for u in range(MOBA_KV_UNROLL)]
        jc = [jnp.minimum(j, nb - 1) for j in js]
        ks = [key_block(j) for j in jc]
        raw = [[jnp.dot(kj, qt_ref[0, e], preferred_element_type=F32) for kj in ks] for e in range(2)]
        probs = []
        for e in range(2):
            m, l, acc = carry[3 * e:3 * e + 3]
            sel = sel_ref[0, e]
            ss = []
            for j, j_in, s in zip(js, jc, raw[e]):
                picked = jnp.logical_and((jnp.right_shift(sel, j_in) & 1) != 0, j < qb)
                ss.append(jnp.where(picked, s, NEG_INF))
            m_new = m
            for s in ss:
                m_new = jnp.maximum(m_new, jnp.max(s, axis=0, keepdims=True))
            alpha = jnp.exp(m - m_new)
            ps = [jnp.exp(s - m_new) for s in ss]
            l = alpha * l
            for p in ps:
                l = l + jnp.sum(p, axis=0, keepdims=True)
            probs.append([p.astype(BF16) for p in ps])
            carry[3 * e:3 * e + 3] = [m_new, l, alpha * acc]
        for e in range(2):
            pv = [jnp.dot(vt_ref[0, e, j], p, preferred_element_type=F32) for j, p in zip(jc, probs[e])]
            carry[3 * e + 2] = carry[3 * e + 2] + sum(pv[1:], pv[0])
        return tuple(carry)

    trips = (qb + MOBA_KV_UNROLL - 1) // MOBA_KV_UNROLL
    state = lax.fori_loop(0, trips, body, tuple(state))
    for e in range(2):
        o_ref[0, e] = state[3 * e + 2] / state[3 * e + 1]


def _moba_attn_call(qt, kn, vt, sel):
    bsz, nh, _, s = qt.shape
    nb = s // MOBA_BLOCK
    return pl.pallas_call(
        _moba_attn_kernel,
        grid=(bsz, nh // 2, nb),
        in_specs=[pl.BlockSpec((1, 2, 2 * HEAD_DIM, MOBA_BLOCK), lambda b, h, i: (b, h, 0, i)),
                  pl.BlockSpec((1, s, 2 * HEAD_DIM), lambda b, h, i: (b, 0, h)),
                  pl.BlockSpec((1, 2, nb, HEAD_DIM, MOBA_BLOCK), lambda b, h, i: (b, h, 0, 0, 0)),
                  pl.BlockSpec((1, 2, 1, MOBA_BLOCK), lambda b, h, i: (b, h, 0, i))],
        out_specs=pl.BlockSpec((1, 2, HEAD_DIM, MOBA_BLOCK), lambda b, h, i: (b, h, 0, i)),
        out_shape=jax.ShapeDtypeStruct((bsz, nh, HEAD_DIM, s), F32),
        compiler_params=pltpu.CompilerParams(
            dimension_semantics=("arbitrary", "arbitrary", "arbitrary"),
            vmem_limit_bytes=VMEM_LIMIT_BYTES),
        name="moba_attn",
    )(qt, kn, vt, sel)


def _moba_branch(q, k, v, q_norm_g, k_norm_g):
    qt, kn, vt, sel = _moba_prep_call(q, k, v, q_norm_g, k_norm_g)
    yt = _moba_attn_call(qt, kn, vt, sel)
    return yt.reshape(yt.shape[0], WIDTH, yt.shape[3])


def _merge_kernel(x_ref, ya_ref, ybt_ref, gates_ref, wb0_ref, wb1_ref, wo_ref, g1_ref, n2_ref, sc2_ref,
                  sh2_ref, wq_ref, x1_ref, h2_ref, q_ref):
    d = x_ref.shape[2]
    ys_a = _mm(ya_ref[0], wb0_ref[...])
    ys_b = _mm_tn(ybt_ref[0], wb1_ref[...])
    gates = gates_ref[0]
    mixed = gates[:, 0:d] * ys_a + gates[:, d:2 * d] * ys_b
    x1 = x_ref[0] + g1_ref[0] * _mm(mixed, wo_ref[...])
    x1_ref[0] = x1
    h2 = _modulated_norm(x1, n2_ref[...], sc2_ref[0], sh2_ref[0])
    h2_ref[0] = h2
    q_ref[0] = _mm(h2, wq_ref[...])


def _merge_call(x, y_a, y_bt, gates, w_b0, w_b1, w_out, gate1, norm2_g, scale2, shift2, w_peer_q):
    bsz, s, d = x.shape
    tm = 256
    nq = w_peer_q.shape[1]
    tok = lambda b, i: (b, i, 0)
    per_b = lambda b, i: (b, 0, 0)
    const = lambda b, i: (0, 0)
    return pl.pallas_call(
        _merge_kernel,
        grid=(bsz, s // tm),
        in_specs=[pl.BlockSpec((1, tm, d), tok),
                  pl.BlockSpec((1, tm, WIDTH), tok),
                  pl.BlockSpec((1, WIDTH, tm), lambda b, i: (b, 0, i)),
                  pl.BlockSpec((1, tm, 2 * d), tok),
                  pl.BlockSpec((WIDTH, d), const),
                  pl.BlockSpec((WIDTH, d), const),
                  pl.BlockSpec((d, d), const),
                  pl.BlockSpec((1, 1, d), per_b),
                  pl.BlockSpec((1, d), const),
                  pl.BlockSpec((1, 1, d), per_b),
                  pl.BlockSpec((1, 1, d), per_b),
                  pl.BlockSpec((d, nq), const)],
        out_specs=[pl.BlockSpec((1, tm, d), tok), pl.BlockSpec((1, tm, d), tok),
                   pl.BlockSpec((1, tm, nq), tok)],
        out_shape=[jax.ShapeDtypeStruct((bsz, s, d), F32), jax.ShapeDtypeStruct((bsz, s, d), F32),
                   jax.ShapeDtypeStruct((bsz, s, nq), F32)],
        compiler_params=pltpu.CompilerParams(dimension_semantics=("arbitrary", "arbitrary"),
                                             vmem_limit_bytes=VMEM_LIMIT_BYTES),
        name="merge",
    )(x, y_a, y_bt, gates, w_b0, w_b1, w_out, gate1, norm2_g.reshape(1, d), scale2, shift2, w_peer_q)


def _topk_rows(x, k, payload=None):
    n = x.shape[0]
    iota = lax.broadcasted_iota(I32, x.shape, 0)
    vals, picks = [], []
    work = x
    for _ in range(k):
        m = jnp.max(work, axis=0, keepdims=True)
        idx = jnp.min(jnp.where(work == m, iota, n), axis=0, keepdims=True)
        hit = iota == idx
        vals.append(m)
        picks.append(idx if payload is None else jnp.max(jnp.where(hit, payload, -1), axis=0, keepdims=True))
        work = jnp.where(hit, -jnp.inf, work)
    return jnp.concatenate(vals, axis=0), jnp.concatenate(picks, axis=0)


def _route_kernel(q_ref, keys_ref, code_ref, gw_ref, *, half_experts):
    k = PEER_TOPK
    eids, gws = [], []
    for hp in range(PEER_HEADS):
        q1 = q_ref[:, (2 * hp) * PEER_KEYS:(2 * hp + 1) * PEER_KEYS]
        q2 = q_ref[:, (2 * hp + 1) * PEER_KEYS:(2 * hp + 2) * PEER_KEYS]
        s1 = _mm_nt(keys_ref[0, hp], q1)
        s2 = _mm_nt(keys_ref[1, hp], q2)
        v1, i1 = _topk_rows(s1, k)
        v2, i2 = _topk_rows(s2, k)
        comb = jnp.concatenate([v1[j:j + 1] + v2 for j in range(k)], axis=0)
        expert = jnp.concatenate([i1[j:j + 1] * PEER_KEYS + i2 for j in range(k)], axis=0)
        sc, eid = _topk_rows(comb, k, payload=expert)
        e = jnp.exp(sc - sc[0:1])
        gws.append(e / jnp.sum(e, axis=0, keepdims=True))
        eids.append(eid)
    eid = jnp.concatenate(eids, axis=0).T
    in_high = eid >= half_experts
    code_ref[...] = jnp.where(in_high, (eid - half_experts) * SUBLANES, eid * SUBLANES + 1)
    gw_ref[...] = jnp.concatenate(gws, axis=0).T


def _route_call(q, sub_keys, n_experts):
    t, nq = q.shape
    tt = 256
    return pl.pallas_call(
        functools.partial(_route_kernel, half_experts=n_experts // 2),
        grid=(t // tt,),
        in_specs=[pl.BlockSpec((tt, nq), lambda i: (i, 0)),
                  pl.BlockSpec(sub_keys.shape, lambda i: (0, 0, 0, 0))],
        out_specs=[pl.BlockSpec((tt, PEER_PICKS), lambda i: (i, 0)),
                   pl.BlockSpec((tt, PEER_PICKS), lambda i: (i, 0))],
        out_shape=[jax.ShapeDtypeStruct((t, PEER_PICKS), I32), jax.ShapeDtypeStruct((t, PEER_PICKS), F32)],
        compiler_params=pltpu.CompilerParams(dimension_semantics=("arbitrary",),
                                             vmem_limit_bytes=VMEM_LIMIT_BYTES),
        name="route",
    )(q, sub_keys)


PEER_TOKENS_PER_STEP = 64
PEER_GROUP = 32
PEER_ACCUMULATORS = 4
_HI_MASK = 0xFFFF0000


def _pack_table(tab):
    n, d = tab.shape
    bits = lax.bitcast_convert_type(tab.astype(BF16), jnp.uint16).astype(jnp.uint32)
    packed = bits[:n // 2] | (bits[n // 2:] << 16)
    return packed.reshape(n // 2 * (d // LANES), LANES)


def _expert_row(tab_ref, code):
    start = pl.multiple_of(code & ~(SUBLANES - 1), SUBLANES)
    word = tab_ref[pl.ds(start, SUBLANES), :]
    shift = (jnp.full(word.shape, code, I32) & 1) << 4
    return lax.bitcast_convert_type(jnp.left_shift(word, shift.astype(jnp.uint32)) & jnp.uint32(_HI_MASK), F32)


def _gelu(x):
    return 0.5 * x * (1.0 + lax.erf(x * (2.0 ** -0.5)))


def _peer_act_kernel(code_ref, h_ref, gw_ref, tab_ref, act_ref, psum_ref):
    tokens = h_ref.shape[0]

    def token(t, carry):
        h = h_ref[t]

        def group(g, carry):
            base = t * PEER_PICKS + g * PEER_GROUP
            for j in range(PEER_GROUP):
                prod = _expert_row(tab_ref, code_ref[base + j]) * h
                psum_ref[pl.ds(base + j, 1), :] = jnp.sum(prod, axis=0, keepdims=True)
            return carry

        lax.fori_loop(0, PEER_PICKS // PEER_GROUP, group, 0)
        return carry

    lax.fori_loop(0, tokens, token, 0)
    ones = jnp.ones((SUBLANES, LANES), BF16)
    dots = jnp.concatenate(
        [_mm_nt(ones, psum_ref[t * PEER_PICKS:(t + 1) * PEER_PICKS, :])[0:1] for t in range(tokens)], axis=0)
    act_ref[...] = _gelu(dots) * gw_ref[...]


def _peer_act_call(code, h2, gw, tab):
    t = h2.shape[0]
    tt = PEER_TOKENS_PER_STEP
    return pl.pallas_call(
        _peer_act_kernel,
        grid=(t // tt,),
        in_specs=[pl.BlockSpec((tt * PEER_PICKS,), lambda i: (i,), memory_space=pltpu.SMEM),
                  pl.BlockSpec((tt, SUBLANES, LANES), lambda i: (i, 0, 0)),
                  pl.BlockSpec((tt, PEER_PICKS), lambda i: (i, 0)),
                  pl.BlockSpec(tab.shape, lambda i: (0, 0), pipeline_mode=pl.Buffered(1))],
        out_specs=pl.BlockSpec((tt, PEER_PICKS), lambda i: (i, 0)),
        out_shape=jax.ShapeDtypeStruct((t, PEER_PICKS), F32),
        scratch_shapes=[pltpu.VMEM((tt * PEER_PICKS, LANES), F32)],
        compiler_params=pltpu.CompilerParams(dimension_semantics=("arbitrary",),
                                             vmem_limit_bytes=VMEM_LIMIT_BYTES),
        name="peer_act",
    )(code, h2, gw, tab)


def _peer_out_kernel(code_ref, act_ref, x_ref, g2_ref, tab_ref, o_ref):
    g2 = g2_ref[0]

    def token(t, carry):
        def group(g, accs):
            accs = list(accs)
            base = t * PEER_PICKS + g * PEER_GROUP
            for j in range(PEER_GROUP):
                accs[j % PEER_ACCUMULATORS] += act_ref[base + j] * _expert_row(tab_ref, code_ref[base + j])
            return tuple(accs)

        zero = jnp.zeros((SUBLANES, LANES), F32)
        accs = lax.fori_loop(0, PEER_PICKS // PEER_GROUP, group, (zero,) * PEER_ACCUMULATORS)
        o_ref[t] = x_ref[t] + g2 * ((accs[0] + accs[1]) + (accs[2] + accs[3]))
        return carry

    lax.fori_loop(0, x_ref.shape[0], token, 0)


def _peer_out_call(code, act, x1, gate2, tab, tokens_per_batch):
    t = x1.shape[0]
    tt = PEER_TOKENS_PER_STEP
    steps_per_batch = tokens_per_batch // tt
    smem = lambda: pl.BlockSpec((tt * PEER_PICKS,), lambda i: (i,), memory_space=pltpu.SMEM)
    return pl.pallas_call(
        _peer_out_kernel,
        grid=(t // tt,),
        in_specs=[smem(), smem(),
                  pl.BlockSpec((tt, SUBLANES, LANES), lambda i: (i, 0, 0)),
                  pl.BlockSpec((1, SUBLANES, LANES), lambda i: (i // steps_per_batch, 0, 0)),
                  pl.BlockSpec(tab.shape, lambda i: (0, 0), pipeline_mode=pl.Buffered(1))],
        out_specs=pl.BlockSpec((tt, SUBLANES, LANES), lambda i: (i, 0, 0)),
        out_shape=jax.ShapeDtypeStruct((t, SUBLANES, LANES), F32),
        compiler_params=pltpu.CompilerParams(dimension_semantics=("arbitrary",),
                                             vmem_limit_bytes=VMEM_LIMIT_BYTES),
        name="peer_out",
    )(code, act, x1, gate2, tab)


def _peer_ffn_residual(x1, h2, q, gate2, sub_keys, peer_u, peer_v):
    bsz, s, d = x1.shape
    t = bsz * s
    code, gw = _route_call(q.reshape(t, -1), sub_keys.astype(BF16), peer_u.shape[0])
    code = code.reshape(-1)
    act = _peer_act_call(code, h2.reshape(t, SUBLANES, LANES), gw, _pack_table(peer_u))
    out = _peer_out_call(code, act.reshape(-1), x1.reshape(t, SUBLANES, LANES),
                         gate2.reshape(bsz, SUBLANES, LANES), _pack_table(peer_v), s)
    return out.reshape(bsz, s, d)


def kernel(x, c, w_ada, b_ada, norm1_g, w_in, mu_rwkv, w0, w2_decay, a0, a2_iclr, g2_gate, k_k, k_a,
           r_k, lnx_w, lnx_b, q_norm_g, k_norm_g, w_branch, w_out, norm2_g, w_peer_q, peer_sub_keys,
           peer_u, peer_v):
    bsz, s, d = x.shape
    ada = _ada_call(c, w_ada, b_ada).reshape(bsz, 6, 1, d)
    shift1, scale1, gate1, shift2, scale2, gate2 = (ada[:, j] for j in range(6))
    w_r = _rwkv_weight_cols(w_in[:, 0:1824]).astype(BF16)
    w_q, w_k, w_v = (w_in[:, 1824 + j * WIDTH:1824 + (j + 1) * WIDTH].astype(BF16) for j in range(3))
    w_g = w_in[:, 1824 + 3 * WIDTH:].astype(BF16)
    p_r, q, k, v, gates = _proj_call(x, norm1_g, scale1, shift1, w_r, w_q, w_k, w_v, w_g)
    y_a = _rwkv_call(p_r, *_rwkv_params(mu_rwkv, w0, w2_decay, a0, a2_iclr, g2_gate, k_k, k_a, r_k,
                                        lnx_w, lnx_b))
    y_bt = _moba_branch(q, k, v, q_norm_g, k_norm_g)
    x1, h2, pq = _merge_call(x, y_a, y_bt, gates, w_branch[0].astype(BF16), w_branch[1].astype(BF16),
                             w_out.astype(BF16), gate1, norm2_g, scale2, shift2, w_peer_q.astype(BF16))
    return _peer_ffn_residual(x1, h2, pq, gate2, peer_sub_keys, peer_u, peer_v)
```

```python
import functools

import jax
import jax.numpy as jnp
from jax import lax
from jax.experimental import pallas as pl
from jax.experimental.pallas import tpu as pltpu

F32 = jnp.float32
BF16 = jnp.bfloat16
I32 = jnp.int32

LANES = 128
SUBLANES = 8
VMEM_LIMIT_BYTES = 56 * 1024 * 1024

NORM_EPS = 1e-6
LNX_EPS = 64e-5
NEG_INF = -1e30

HEAD_DIM = 64
N_HEADS = 8
WIDTH = N_HEADS * HEAD_DIM
RWKV_CHUNK = 64
RWKV_TILE = 128
RWKV_PCOLS = 2048
MOBA_BLOCK = 256
MOBA_TOPK = 3
PEER_HEADS = 8
PEER_KEYS = 128
PEER_TOPK = 16
PEER_PICKS = PEER_HEADS * PEER_TOPK


def _mm(a, b):
    return jnp.dot(a.astype(BF16), b.astype(BF16), preferred_element_type=F32)


def _mm_nt(a, b):
    return lax.dot_general(a.astype(BF16), b.astype(BF16), (((1,), (1,)), ((), ())),
                           preferred_element_type=F32)


def _mm_tn(a, b):
    return lax.dot_general(a.astype(BF16), b.astype(BF16), (((0,), (0,)), ((), ())),
                           preferred_element_type=F32)


def _mm_split(a, b):
    hi = a.astype(BF16)
    lo = (a - hi.astype(F32)).astype(BF16)
    b = b.astype(BF16)
    return (jnp.dot(hi, b, preferred_element_type=F32) + jnp.dot(lo, b, preferred_element_type=F32))


def _mmh(a, b):
    return jnp.dot(a, b, preferred_element_type=F32, precision=lax.Precision.HIGHEST)


def _head_ones(n):
    r = lax.broadcasted_iota(I32, (n, n), 0) // HEAD_DIM
    c = lax.broadcasted_iota(I32, (n, n), 1) // HEAD_DIM
    return (r == c).astype(F32)


def _sigmoid(x):
    return 1.0 / (1.0 + jnp.exp(-x))


def _ada_kernel(c_ref, w_ref, b_ref, o_ref):
    c = c_ref[...]
    o_ref[...] = _mmh(c * _sigmoid(c), w_ref[...]) + b_ref[...]


def _ada_call(c, w_ada, b_ada):
    bsz, d = c.shape
    n = w_ada.shape[1]
    tn = 1536
    return pl.pallas_call(
        _ada_kernel,
        grid=(n // tn,),
        in_specs=[pl.BlockSpec((bsz, d), lambda j: (0, 0)),
                  pl.BlockSpec((d, tn), lambda j: (0, j)),
                  pl.BlockSpec((1, tn), lambda j: (0, j))],
        out_specs=pl.BlockSpec((bsz, tn), lambda j: (0, j)),
        out_shape=jax.ShapeDtypeStruct((bsz, n), F32),
        compiler_params=pltpu.CompilerParams(dimension_semantics=("arbitrary",),
                                             vmem_limit_bytes=VMEM_LIMIT_BYTES),
        name="ada",
    )(c, w_ada, b_ada.reshape(1, n))


def _rwkv_kernel(p_ref, pprev_ref, mu_ref, vec_ref, w2_ref, a2_ref, g2_ref, y_ref, st_ref):
    i = pl.program_id(1)
    tile = p_ref.shape[1]
    L = RWKV_CHUNK

    @pl.when(i == 0)
    def _():
        st_ref[...] = jnp.zeros_like(st_ref)

    p = p_ref[0]
    prev_last = jnp.where(i == 0, 0.0, pprev_ref[0][SUBLANES - 1:SUBLANES, :])
    row = lax.broadcasted_iota(I32, p.shape, 0)
    p_prev = jnp.where(row == 0, prev_last, pltpu.roll(p, 1, 0))
    xm = p + (p_prev - p) * mu_ref[...]
    r = xm[:, 0:WIDTH]
    k = xm[:, WIDTH:2 * WIDTH]
    v = xm[:, 2 * WIDTH:3 * WIDTH]
    wl = xm[:, 3 * WIDTH:3 * WIDTH + 128]
    al = xm[:, 3 * WIDTH + 128:3 * WIDTH + 256]
    gl = xm[:, 3 * WIDTH + 256:3 * WIDTH + 512]
    w0, a0, k_k, k_a = vec_ref[0:1, :], vec_ref[1:2, :], vec_ref[2:3, :], vec_ref[3:4, :]
    r_k, lnx_w, lnx_b = vec_ref[4:5, :], vec_ref[5:6, :], vec_ref[6:7, :]

    z = -(w0 + _mmh(jnp.tanh(wl), w2_ref[...]))
    softplus = jnp.maximum(z, 0.0) + jnp.log(1.0 + jnp.exp(-jnp.abs(z)))
    lw = -jnp.exp(-softplus - 0.5)
    a = _sigmoid(a0 + _mmh(al, a2_ref[...]))
    g = _mmh(_sigmoid(gl), g2_ref[...])
    hones = _head_ones(WIDTH)
    kk = k * k_k
    kk = kk / jnp.maximum(jnp.sqrt(_mm(kk * kk, hones)), 1e-12)
    k2 = k * (1.0 + (a - 1.0) * k_a)
    a_s = -kk
    b_s = kk * a

    li = lax.broadcasted_iota(I32, (L, L), 0)
    lj = lax.broadcasted_iota(I32, (L, L), 1)
    tril_incl = li >= lj
    tril_strict = li > lj
    eye = (li == lj).astype(F32)
    csum = tril_incl.astype(F32)

    n_chunks = tile // L
    items = [(c, h) for c in range(n_chunks) for h in range(N_HEADS)]
    zeros_ll = jnp.zeros((L, L), F32)
    per_chunk = []
    for c in range(n_chunks):
        sl = slice(c * L, (c + 1) * L)
        lw_c = lw[sl]
        cum = _mmh(csum, lw_c)
        ref = cum[L // 2 - 1:L // 2, :]
        g_inv = jnp.exp(ref - cum)
        g_end = jnp.exp(cum[L - 1:L, :] - ref)
        bt = b_s[sl] * g_inv
        kt = k2[sl] * g_inv
        per_chunk.append(dict(
            at=a_s[sl] * jnp.exp(cum - lw_c - ref), rt=r[sl] * jnp.exp(cum - ref), bt=bt, kt=kt,
            bbar=bt * g_end, kbar=kt * g_end, v=v[sl], g_tot=jnp.exp(cum[L - 1:L, :]), g_ref=jnp.exp(ref)))

    def head(c, h, name):
        return per_chunk[c][name][:, h * HEAD_DIM:(h + 1) * HEAD_DIM]

    sc = [_mm_nt(jnp.concatenate([head(c, h, "at"), head(c, h, "rt")], axis=0),
                 jnp.concatenate([head(c, h, "bt"), head(c, h, "kt")], axis=0)) for c, h in items]
    m_ab = [jnp.where(tril_strict, s[0:L, 0:L], 0.0) for s in sc]
    m_ak = [jnp.where(tril_strict, s[0:L, L:2 * L], 0.0) for s in sc]
    tril_incl2 = (lax.broadcasted_iota(I32, (L, 2 * L), 0)
                  >= lax.broadcasted_iota(I32, (L, 2 * L), 1) % L)
    g_rbk = [jnp.where(tril_incl2, s[L:2 * L, :], 0.0) for s in sc]
    akv = [_mm(m, head(c, h, "v")) for m, (c, h) in zip(m_ak, items)]
    pw = [_mm(m, m) for m in m_ab]
    t_inv = [eye + m for m in m_ab]
    for _ in range(4):
        both = [_mm(jnp.concatenate([p, t], axis=0), p) for p, t in zip(pw, t_inv)]
        t_inv = [t + b[L:2 * L] for t, b in zip(t_inv, both)]
        pw = [b[0:L] for b in both]
    t_inv = [t + _mm(t, p) for t, p in zip(t_inv, pw)]
    tw = [_mm(t, jnp.concatenate([head(c, h, "at"), x], axis=1)) for t, x, (c, h) in zip(t_inv, akv, items)]
    ro = [_mm(g, jnp.concatenate([w, jnp.concatenate([zeros_ll, head(c, h, "v")], axis=1)], axis=0))
          for g, w, (c, h) in zip(g_rbk, tw, items)]
    a_hat = [w[:, 0:L] * head(c, h, "g_ref") for w, (c, h) in zip(tw, items)]
    r_hat = [(head(c, h, "rt") + x[:, 0:L]) * head(c, h, "g_ref") for x, (c, h) in zip(ro, items)]
    pp = [_mm_tn(jnp.concatenate([head(c, h, "bbar"), head(c, h, "kbar")], axis=0),
                 jnp.concatenate([jnp.concatenate([a, w[:, L:2 * L]], axis=1),
                                  jnp.concatenate([zeros_ll, head(c, h, "v")], axis=1)], axis=0))
          for a, w, (c, h) in zip(a_hat, tw, items)]
    state = [st_ref[h] for h in range(N_HEADS)]
    outs = []
    for c in range(n_chunks):
        head_outs = []
        for h in range(N_HEADS):
            i = c * N_HEADS + h
            phi_t = eye * head(c, h, "g_tot") + pp[i][:, 0:L]
            both = _mm(jnp.concatenate([r_hat[i], phi_t], axis=0), state[h])
            head_outs.append(both[0:L] + ro[i][:, L:2 * L])
            state[h] = both[L:2 * L] + pp[i][:, L:2 * L]
        outs.append(jnp.concatenate(head_outs, axis=1))
    for h in range(N_HEADS):
        st_ref[h] = state[h]
    out = jnp.concatenate(outs, axis=0)

    inv_n = 1.0 / HEAD_DIM
    mean = _mm(out, hones) * inv_n
    cen = out - mean
    var = _mm(cen * cen, hones) * inv_n
    out = cen * lax.rsqrt(var + LNX_EPS) * lnx_w + lnx_b
    bonus = _mm(r * k2 * r_k, hones) * v
    y_ref[0] = (out + bonus) * g


def _mmh_nt(a, b):
    return lax.dot_general(a, b, (((1,), (1,)), ((), ())), preferred_element_type=F32,
                           precision=lax.Precision.HIGHEST)


def _mmh_tn(a, b):
    return lax.dot_general(a, b, (((0,), (0,)), ((), ())), preferred_element_type=F32,
                           precision=lax.Precision.HIGHEST)


def _rwkv_call(p, mu, vecs, w2, a2, g2):
    bsz, s, pc = p.shape
    tile = RWKV_TILE
    const = lambda b, i: (0, 0)
    return pl.pallas_call(
        _rwkv_kernel,
        grid=(bsz, s // tile),
        in_specs=[pl.BlockSpec((1, tile, pc), lambda b, i: (b, i, 0)),
                  pl.BlockSpec((1, SUBLANES, pc),
                               lambda b, i: (b, jnp.maximum(i * (tile // SUBLANES) - 1, 0), 0)),
                  pl.BlockSpec((1, pc), const),
                  pl.BlockSpec((SUBLANES, WIDTH), const),
                  pl.BlockSpec((128, WIDTH), const),
                  pl.BlockSpec((128, WIDTH), const),
                  pl.BlockSpec((256, WIDTH), const)],
        out_specs=pl.BlockSpec((1, tile, WIDTH), lambda b, i: (b, i, 0)),
        out_shape=jax.ShapeDtypeStruct((bsz, s, WIDTH), F32),
        scratch_shapes=[pltpu.VMEM((N_HEADS, HEAD_DIM, HEAD_DIM), F32)],
        compiler_params=pltpu.CompilerParams(dimension_semantics=("arbitrary", "arbitrary"),
                                             vmem_limit_bytes=VMEM_LIMIT_BYTES),
        name="rwkv",
    )(p, p, mu, vecs, w2, a2, g2)


def _pad_rows(w, n):
    return jnp.pad(w, ((0, n - w.shape[0]), (0, 0)))


def _rwkv_params(mu_rwkv, w0, w2_decay, a0, a2_iclr, g2_gate, k_k, k_a, r_k, lnx_w, lnx_b):
    mu = jnp.concatenate([mu_rwkv[0:1536], _pad1(mu_rwkv[1536:1600], 128),
                          _pad1(mu_rwkv[1600:1664], 128), _pad1(mu_rwkv[1664:1824], 256)])
    vecs = jnp.stack([w0, a0, k_k, k_a, r_k.reshape(-1), lnx_w, lnx_b, jnp.zeros_like(w0)])
    return (mu.reshape(1, RWKV_PCOLS), vecs, _pad_rows(w2_decay, 128), _pad_rows(a2_iclr, 128),
            _pad_rows(g2_gate, 256))


def _pad1(v, n):
    return jnp.pad(v, (0, n - v.shape[0]))


def _pad_last(x, n):
    return jnp.pad(x, [(0, 0)] * (x.ndim - 1) + [(0, n - x.shape[-1])])


def _pad_cols(w, n):
    return _pad_last(w, n)


def _rwkv_weight_cols(w):
    return jnp.concatenate([w[:, 0:1536], _pad_cols(w[:, 1536:1600], 128),
                            _pad_cols(w[:, 1600:1664], 128), _pad_cols(w[:, 1664:1824], 256)], axis=1)


def _modulated_norm(x, gain, scale, shift):
    y = x * lax.rsqrt(jnp.mean(x * x, axis=-1, keepdims=True) + NORM_EPS)
    return y * gain * (1.0 + scale) + shift


def _proj_kernel(x_ref, g_ref, sc_ref, sh_ref, wr_ref, wq_ref, wk_ref, wv_ref, wg_ref,
                 pr_ref, q_ref, k_ref, v_ref, gate_ref):
    h = _modulated_norm(x_ref[0], g_ref[...], sc_ref[0], sh_ref[0]).astype(BF16)
    pr_ref[0] = jnp.dot(h, wr_ref[...], preferred_element_type=F32)
    q_ref[0] = jnp.dot(h, wq_ref[...], preferred_element_type=F32)
    k_ref[0] = jnp.dot(h, wk_ref[...], preferred_element_type=F32)
    v_ref[0] = jnp.dot(h, wv_ref[...], preferred_element_type=F32)
    gate_ref[0] = _sigmoid(jnp.dot(h, wg_ref[...], preferred_element_type=F32))


def _proj_call(x, norm_g, scale, shift, w_r, w_q, w_k, w_v, w_g):
    bsz, s, d = x.shape
    tm = 512
    tok = lambda b, i: (b, i, 0)
    per_b = lambda b, i: (b, 0, 0)
    const = lambda b, i: (0, 0)
    outs = [(RWKV_PCOLS, F32), (WIDTH, F32), (WIDTH, F32), (WIDTH, F32), (2 * d, F32)]
    return pl.pallas_call(
        _proj_kernel,
        grid=(bsz, s // tm),
        in_specs=[pl.BlockSpec((1, tm, d), tok),
                  pl.BlockSpec((1, d), const),
                  pl.BlockSpec((1, 1, d), per_b),
                  pl.BlockSpec((1, 1, d), per_b)]
                 + [pl.BlockSpec(w.shape, const) for w in (w_r, w_q, w_k, w_v, w_g)],
        out_specs=[pl.BlockSpec((1, tm, n), tok) for n, _ in outs],
        out_shape=[jax.ShapeDtypeStruct((bsz, s, n), dt) for n, dt in outs],
        compiler_params=pltpu.CompilerParams(dimension_semantics=("arbitrary", "arbitrary"),
                                             vmem_limit_bytes=VMEM_LIMIT_BYTES),
        name="proj",
    )(x, norm_g.reshape(1, d), scale, shift, w_r, w_q, w_k, w_v, w_g)


MOBA_PREP_TILE = 2 * MOBA_BLOCK


def _moba_prep_kernel(q_ref, k_ref, v_ref, qg_ref, kg_ref, qt_ref, kn_ref, vt_ref, sel_ref, kmean_ref):
    i = pl.program_id(1)
    tile = q_ref.shape[1]
    nb = kmean_ref.shape[0]
    blocks_per_tile = tile // MOBA_BLOCK

    @pl.when(i == 0)
    def _():
        kmean_ref[...] = jnp.zeros_like(kmean_ref)

    q, k, v = q_ref[0], k_ref[0], v_ref[0]
    hones = _head_ones(WIDTH)
    inv_n = 1.0 / HEAD_DIM
    qn = q * lax.rsqrt(_mm_split(q * q, hones) * inv_n + NORM_EPS) * qg_ref[...] * (HEAD_DIM ** -0.5)
    kn = k * lax.rsqrt(_mm_split(k * k, hones) * inv_n + NORM_EPS) * kg_ref[...]
    kn_ref[0] = kn.astype(BF16)

    kmean = kmean_ref[...]
    brow = lax.broadcasted_iota(I32, kmean.shape, 0)
    for blk in range(blocks_per_tile):
        km = jnp.mean(kn[blk * MOBA_BLOCK:(blk + 1) * MOBA_BLOCK], axis=0, keepdims=True)
        kmean = jnp.where(brow == i * blocks_per_tile + blk, km, kmean)
    kmean_ref[...] = kmean

    qnt = qn.T
    vt = v.T
    half = lax.broadcasted_iota(I32, (2 * HEAD_DIM, tile), 0) // HEAD_DIM
    n_iota = lax.broadcasted_iota(I32, (nb, tile), 0)
    q_blk = i * blocks_per_tile + lax.broadcasted_iota(I32, (nb, tile), 1) // MOBA_BLOCK
    for h in range(N_HEADS):
        pair = qnt[(h // 2) * 2 * HEAD_DIM:(h // 2 + 1) * 2 * HEAD_DIM]
        qt_ref[0, h] = jnp.where(half == h % 2, pair, 0.0).astype(BF16)
        for blk in range(blocks_per_tile):
            vt_ref[0, h, blk] = vt[h * HEAD_DIM:(h + 1) * HEAD_DIM,
                                   blk * MOBA_BLOCK:(blk + 1) * MOBA_BLOCK].astype(BF16)
        hs = slice(h * HEAD_DIM, (h + 1) * HEAD_DIM)
        gate = _mmh_nt(kmean[:, hs], qn[:, hs])
        work = jnp.where(n_iota < q_blk, gate, -jnp.inf)
        bits = jnp.zeros((1, tile), I32)
        for _ in range(MOBA_TOPK):
            m = jnp.max(work, axis=0, keepdims=True)
            idx = jnp.min(jnp.where(work == m, n_iota, nb), axis=0, keepdims=True)
            hit = n_iota == idx
            take = jnp.logical_and(hit, m > -jnp.inf)
            bits = bits | jnp.sum(jnp.where(take, jnp.left_shift(1, n_iota), 0), axis=0, keepdims=True)
            work = jnp.where(hit, -jnp.inf, work)
        sel_ref[0, h] = bits


def _moba_prep_call(q, k, v, q_norm_g, k_norm_g):
    bsz, s, w = q.shape
    tile = MOBA_PREP_TILE
    nb = s // MOBA_BLOCK
    assert nb <= 32, "block selection is a 32-bit mask"
    tok = lambda b, i: (b, i, 0)
    const = lambda b, i: (0, 0)
    return pl.pallas_call(
        _moba_prep_kernel,
        grid=(bsz, s // tile),
        in_specs=[pl.BlockSpec((1, tile, w), tok)] * 3 + [pl.BlockSpec((1, w), const)] * 2,
        out_specs=[pl.BlockSpec((1, N_HEADS, 2 * HEAD_DIM, tile), lambda b, i: (b, 0, 0, i)),
                   pl.BlockSpec((1, tile, w), tok),
                   pl.BlockSpec((1, N_HEADS, tile // MOBA_BLOCK, HEAD_DIM, MOBA_BLOCK),
                                lambda b, i: (b, 0, i, 0, 0)),
                   pl.BlockSpec((1, N_HEADS, 1, tile), lambda b, i: (b, 0, 0, i))],
        out_shape=[jax.ShapeDtypeStruct((bsz, N_HEADS, 2 * HEAD_DIM, s), BF16),
                   jax.ShapeDtypeStruct((bsz, s, w), BF16),
                   jax.ShapeDtypeStruct((bsz, N_HEADS, nb, HEAD_DIM, MOBA_BLOCK), BF16),
                   jax.ShapeDtypeStruct((bsz, N_HEADS, 1, s), I32)],
        scratch_shapes=[pltpu.VMEM((nb, w), F32)],
        compiler_params=pltpu.CompilerParams(dimension_semantics=("arbitrary", "arbitrary"),
                                             vmem_limit_bytes=VMEM_LIMIT_BYTES),
        name="moba_prep",
    )(q, k, v, jnp.tile(q_norm_g, N_HEADS).reshape(1, w), jnp.tile(k_norm_g, N_HEADS).reshape(1, w))


MOBA_KV_UNROLL = 2


def _moba_attn_kernel(qt_ref, k_ref, vt_ref, sel_ref, o_ref):
    qb = pl.program_id(2)
    bl = MOBA_BLOCK
    nb = vt_ref.shape[2]
    kpos = lax.broadcasted_iota(I32, (bl, bl), 0)
    qpos = lax.broadcasted_iota(I32, (bl, bl), 1)

    def key_block(j):
        return k_ref[0, pl.ds(pl.multiple_of(j * bl, bl), bl), :]

    k_own = key_block(qb)
    state = []
    for e in range(2):
        s = jnp.where(kpos <= qpos, jnp.dot(k_own, qt_ref[0, e], preferred_element_type=F32), NEG_INF)
        m = jnp.max(s, axis=0, keepdims=True)
        p = jnp.exp(s - m)
        l = jnp.sum(p, axis=0, keepdims=True)
        acc = jnp.dot(vt_ref[0, e, qb], p.astype(BF16), preferred_element_type=F32)
        state += [m, l, acc]

    def body(it, carry):
        carry = list(carry)
        js = [it * MOBA_KV_UNROLL + u for u in range(MOBA_KV_UNROLL)]
        jc = [jnp.minimum(j, nb - 1) for j in js]
        ks = [key_block(j) for j in jc]
        raw = [[jnp.dot(kj, qt_ref[0, e], preferred_element_type=F32) for kj in ks] for e in range(2)]
        probs = []
        for e in range(2):
            m, l, acc = carry[3 * e:3 * e + 3]
            sel = sel_ref[0, e]
            ss = []
            for j, j_in, s in zip(js, jc, raw[e]):
                picked = jnp.logical_and((jnp.right_shift(sel, j_in) & 1) != 0, j < qb)
                ss.append(jnp.where(picked, s, NEG_INF))
            m_new = m
            for s in ss:
                m_new = jnp.maximum(m_new, jnp.max(s, axis=0, keepdims=True))
            alpha = jnp.exp(m - m_new)
            ps = [jnp.exp(s - m_new) for s in ss]
            l = alpha * l
            for p in ps:
                l = l + jnp.sum(p, axis=0, keepdims=True)
            probs.append([p.astype(BF16) for p in ps])
            carry[3 * e:3 * e + 3] = [m_new, l, alpha * acc]
        for e in range(2):
            pv = [jnp.dot(vt_ref[0, e, j], p, preferred_element_type=F32) for j, p in zip(jc, probs[e])]
            carry[3 * e + 2] = carry[3 * e + 2] + sum(pv[1:], pv[0])
        return tuple(carry)

    trips = (qb + MOBA_KV_UNROLL - 1) // MOBA_KV_UNROLL
    state = lax.fori_loop(0, trips, body, tuple(state))
    for e in range(2):
        o_ref[0, e] = state[3 * e + 2] / state[3 * e + 1]


def _moba_attn_call(qt, kn, vt, sel):
    bsz, nh, _, s = qt.shape
    nb = s // MOBA_BLOCK
    return pl.pallas_call(
        _moba_attn_kernel,
        grid=(bsz, nh // 2, nb),
        in_specs=[pl.BlockSpec((1, 2, 2 * HEAD_DIM, MOBA_BLOCK), lambda b, h, i: (b, h, 0, i)),
                  pl.BlockSpec((1, s, 2 * HEAD_DIM), lambda b, h, i: (b, 0, h)),
                  pl.BlockSpec((1, 2, nb, HEAD_DIM, MOBA_BLOCK), lambda b, h, i: (b, h, 0, 0, 0)),
                  pl.BlockSpec((1, 2, 1, MOBA_BLOCK), lambda b, h, i: (b, h, 0, i))],
        out_specs=pl.BlockSpec((1, 2, HEAD_DIM, MOBA_BLOCK), lambda b, h, i: (b, h, 0, i)),
        out_shape=jax.ShapeDtypeStruct((bsz, nh, HEAD_DIM, s), F32),
        compiler_params=pltpu.CompilerParams(
            dimension_semantics=("arbitrary", "arbitrary", "arbitrary"),
            vmem_limit_bytes=VMEM_LIMIT_BYTES),
        name="moba_attn",
    )(qt, kn, vt, sel)


def _moba_branch(q, k, v, q_norm_g, k_norm_g):
    qt, kn, vt, sel = _moba_prep_call(q, k, v, q_norm_g, k_norm_g)
    yt = _moba_attn_call(qt, kn, vt, sel)
    return yt.reshape(yt.shape[0], WIDTH, yt.shape[3])


def _merge_kernel(x_ref, ya_ref, ybt_ref, gates_ref, wb0_ref, wb1_ref, wo_ref, g1_ref, n2_ref, sc2_ref,
                  sh2_ref, wq_ref, x1_ref, h2_ref, q_ref):
    d = x_ref.shape[2]
    ys_a = _mm(ya_ref[0], wb0_ref[...])
    ys_b = _mm_tn(ybt_ref[0], wb1_ref[...])
    gates = gates_ref[0]
    mixed = gates[:, 0:d] * ys_a + gates[:, d:2 * d] * ys_b
    x1 = x_ref[0] + g1_ref[0] * _mm(mixed, wo_ref[...])
    x1_ref[0] = x1
    h2 = _modulated_norm(x1, n2_ref[...], sc2_ref[0], sh2_ref[0])
    h2_ref[0] = h2
    q_ref[0] = _mm(h2, wq_ref[...])


def _merge_call(x, y_a, y_bt, gates, w_b0, w_b1, w_out, gate1, norm2_g, scale2, shift2, w_peer_q):
    bsz, s, d = x.shape
    tm = 256
    nq = w_peer_q.shape[1]
    tok = lambda b, i: (b, i, 0)
    per_b = lambda b, i: (b, 0, 0)
    const = lambda b, i: (0, 0)
    return pl.pallas_call(
        _merge_kernel,
        grid=(bsz, s // tm),
        in_specs=[pl.BlockSpec((1, tm, d), tok),
                  pl.BlockSpec((1, tm, WIDTH), tok),
                  pl.BlockSpec((1, WIDTH, tm), lambda b, i: (b, 0, i)),
                  pl.BlockSpec((1, tm, 2 * d), tok),
                  pl.BlockSpec((WIDTH, d), const),
                  pl.BlockSpec((WIDTH, d), const),
                  pl.BlockSpec((d, d), const),
                  pl.BlockSpec((1, 1, d), per_b),
                  pl.BlockSpec((1, d), const),
                  pl.BlockSpec((1, 1, d), per_b),
                  pl.BlockSpec((1, 1, d), per_b),
                  pl.BlockSpec((d, nq), const)],
        out_specs=[pl.BlockSpec((1, tm, d), tok), pl.BlockSpec((1, tm, d), tok),
                   pl.BlockSpec((1, tm, nq), tok)],
        out_shape=[jax.ShapeDtypeStruct((bsz, s, d), F32), jax.ShapeDtypeStruct((bsz, s, d), F32),
                   jax.ShapeDtypeStruct((bsz, s, nq), F32)],
        compiler_params=pltpu.CompilerParams(dimension_semantics=("arbitrary", "arbitrary"),
                                             vmem_limit_bytes=VMEM_LIMIT_BYTES),
        name="merge",
    )(x, y_a, y_bt, gates, w_b0, w_b1, w_out, gate1, norm2_g.reshape(1, d), scale2, shift2, w_peer_q)


def _topk_rows(x, k, payload=None):
    n = x.shape[0]
    iota = lax.broadcasted_iota(I32, x.shape, 0)
    vals, picks = [], []
    work = x
    for _ in range(k):
        m = jnp.max(work, axis=0, keepdims=True)
        idx = jnp.min(jnp.where(work == m, iota, n), axis=0, keepdims=True)
        hit = iota == idx
        vals.append(m)
        picks.append(idx if payload is None else jnp.max(jnp.where(hit, payload, -1), axis=0, keepdims=True))
        work = jnp.where(hit, -jnp.inf, work)
    return jnp.concatenate(vals, axis=0), jnp.concatenate(picks, axis=0)


def _route_kernel(q_ref, keys_ref, code_ref, gw_ref):
    k = PEER_TOPK
    eids, gws = [], []
    for hp in range(PEER_HEADS):
        q1 = q_ref[:, (2 * hp) * PEER_KEYS:(2 * hp + 1) * PEER_KEYS]
        q2 = q_ref[:, (2 * hp + 1) * PEER_KEYS:(2 * hp + 2) * PEER_KEYS]
        s1 = _mm_nt(keys_ref[0, hp], q1)
        s2 = _mm_nt(keys_ref[1, hp], q2)
        v1, i1 = _topk_rows(s1, k)
        v2, i2 = _topk_rows(s2, k)
        comb = jnp.concatenate([v1[j:j + 1] + v2 for j in range(k)], axis=0)
        expert = jnp.concatenate([i1[j:j + 1] * PEER_KEYS + i2 for j in range(k)], axis=0)
        sc, eid = _topk_rows(comb, k, payload=expert)
        e = jnp.exp(sc - sc[0:1])
        gws.append(e / jnp.sum(e, axis=0, keepdims=True))
        eids.append(eid)
    code_ref[...] = jnp.concatenate(eids, axis=0).T
    gw_ref[...] = jnp.concatenate(gws, axis=0).T


def _route_call(q, sub_keys):
    t, nq = q.shape
    tt = 256
    return pl.pallas_call(
        _route_kernel,
        grid=(t // tt,),
        in_specs=[pl.BlockSpec((tt, nq), lambda i: (i, 0)),
                  pl.BlockSpec(sub_keys.shape, lambda i: (0, 0, 0, 0))],
        out_specs=[pl.BlockSpec((tt, PEER_PICKS), lambda i: (i, 0)),
                   pl.BlockSpec((tt, PEER_PICKS), lambda i: (i, 0))],
        out_shape=[jax.ShapeDtypeStruct((t, PEER_PICKS), I32), jax.ShapeDtypeStruct((t, PEER_PICKS), F32)],
        compiler_params=pltpu.CompilerParams(dimension_semantics=("arbitrary",),
                                             vmem_limit_bytes=VMEM_LIMIT_BYTES),
        name="route",
    )(q, sub_keys)


PEER_TOKENS_PER_STEP = 64
PEER_ACCUMULATORS = 4


def _tile_table(tab):
    n, d = tab.shape
    return tab.astype(BF16).reshape(n, d // LANES, LANES)


def _gelu(x):
    return 0.5 * x * (1.0 + lax.erf(x * (2.0 ** -0.5)))


def _sublane_sums(tiles):
    row = lax.broadcasted_iota(I32, (SUBLANES, LANES), 0)
    odd, upper_pair, upper_half = (row & 1) != 0, (row & 2) != 0, row >= 4

    def quad(a, b, c, d):
        ab = jnp.where(odd, a + pltpu.roll(a, 1, 0), b + pltpu.roll(b, SUBLANES - 1, 0))
        cd = jnp.where(odd, c + pltpu.roll(c, 1, 0), d + pltpu.roll(d, SUBLANES - 1, 0))
        q = jnp.where(upper_pair, ab + pltpu.roll(ab, 2, 0), cd + pltpu.roll(cd, SUBLANES - 2, 0))
        return q + pltpu.roll(q, 4, 0)

    return jnp.where(upper_half, quad(tiles[7], tiles[6], tiles[5], tiles[4]),
                     quad(tiles[3], tiles[2], tiles[1], tiles[0]))


def _peer_act_kernel(code_ref, h_ref, gw_ref, tab_ref, act_ref, psum_ref):
    tokens = h_ref.shape[0]

    def token(t, carry):
        h = h_ref[t]
        base = pl.multiple_of(t * PEER_PICKS, PEER_PICKS)
        for j0 in range(0, PEER_PICKS, SUBLANES):
            prods = [tab_ref[code_ref[base + j0 + j]].astype(F32) * h for j in range(SUBLANES)]
            psum_ref[pl.ds(base + j0, SUBLANES), :] = _sublane_sums(prods)
        return carry

    lax.fori_loop(0, tokens, token, 0)
    ones = jnp.ones((SUBLANES, LANES), BF16)
    dots = jnp.concatenate(
        [_mm_nt(ones, psum_ref[t * PEER_PICKS:(t + 1) * PEER_PICKS, :])[0:1] for t in range(tokens)], axis=0)
    act_ref[...] = _gelu(dots) * gw_ref[...]


def _peer_act_call(code, h2, gw, tab):
    t = h2.shape[0]
    tt = PEER_TOKENS_PER_STEP
    return pl.pallas_call(
        _peer_act_kernel,
        grid=(t // tt,),
        in_specs=[pl.BlockSpec((tt * PEER_PICKS,), lambda i: (i,), memory_space=pltpu.SMEM),
                  pl.BlockSpec((tt, SUBLANES, LANES), lambda i: (i, 0, 0)),
                  pl.BlockSpec((tt, PEER_PICKS), lambda i: (i, 0)),
                  pl.BlockSpec(tab.shape, lambda i: (0, 0, 0), pipeline_mode=pl.Buffered(1))],
        out_specs=pl.BlockSpec((tt, PEER_PICKS), lambda i: (i, 0)),
        out_shape=jax.ShapeDtypeStruct((t, PEER_PICKS), F32),
        scratch_shapes=[pltpu.VMEM((tt * PEER_PICKS, LANES), F32)],
        compiler_params=pltpu.CompilerParams(dimension_semantics=("arbitrary",),
                                             vmem_limit_bytes=VMEM_LIMIT_BYTES),
        name="peer_act",
    )(code, h2, gw, tab)


def _peer_out_kernel(code_ref, act_ref, x_ref, g2_ref, tab_ref, o_ref, wrep_ref):
    tokens = x_ref.shape[0]
    g2 = g2_ref[0]
    eye = (lax.broadcasted_iota(I32, (PEER_PICKS, PEER_PICKS), 0)
           == lax.broadcasted_iota(I32, (PEER_PICKS, PEER_PICKS), 1)).astype(F32)
    ones = jnp.ones((PEER_PICKS, LANES), BF16)
    for t in range(tokens):
        wrep_ref[t * PEER_PICKS:(t + 1) * PEER_PICKS, :] = _mm_split(eye * act_ref[t:t + 1, :], ones)

    def token(t, carry):
        accs = [jnp.zeros((SUBLANES, LANES), F32) for _ in range(PEER_ACCUMULATORS)]
        for s in range(PEER_PICKS):
            w = jnp.broadcast_to(wrep_ref[pl.ds(t * PEER_PICKS + s, 1), :], (SUBLANES, LANES))
            accs[s % PEER_ACCUMULATORS] += w * tab_ref[code_ref[t * PEER_PICKS + s]].astype(F32)
        o_ref[t] = x_ref[t] + g2 * ((accs[0] + accs[1]) + (accs[2] + accs[3]))
        return carry

    lax.fori_loop(0, tokens, token, 0)


def _peer_out_call(code, act, x1, gate2, tab, tokens_per_batch):
    t = x1.shape[0]
    tt = PEER_TOKENS_PER_STEP
    steps_per_batch = tokens_per_batch // tt
    return pl.pallas_call(
        _peer_out_kernel,
        grid=(t // tt,),
        in_specs=[pl.BlockSpec((tt * PEER_PICKS,), lambda i: (i,), memory_space=pltpu.SMEM),
                  pl.BlockSpec((tt, PEER_PICKS), lambda i: (i, 0)),
                  pl.BlockSpec((tt, SUBLANES, LANES), lambda i: (i, 0, 0)),
                  pl.BlockSpec((1, SUBLANES, LANES), lambda i: (i // steps_per_batch, 0, 0)),
                  pl.BlockSpec(tab.shape, lambda i: (0, 0, 0), pipeline_mode=pl.Buffered(1))],
        out_specs=pl.BlockSpec((tt, SUBLANES, LANES), lambda i: (i, 0, 0)),
        out_shape=jax.ShapeDtypeStruct((t, SUBLANES, LANES), F32),
        scratch_shapes=[pltpu.VMEM((tt * PEER_PICKS, LANES), F32)],
        compiler_params=pltpu.CompilerParams(dimension_semantics=("arbitrary",),
                                             vmem_limit_bytes=VMEM_LIMIT_BYTES),
        name="peer_out",
    )(code, act, x1, gate2, tab)


def _peer_ffn_residual(x1, h2, q, gate2, sub_keys, peer_u, peer_v):
    bsz, s, d = x1.shape
    t = bsz * s
    code, gw = _route_call(q.reshape(t, -1), sub_keys.astype(BF16))
    code = code.reshape(-1)
    act = _peer_act_call(code, h2.reshape(t, SUBLANES, LANES), gw, _tile_table(peer_u))
    out = _peer_out_call(code, act, x1.reshape(t, SUBLANES, LANES),
                         gate2.reshape(bsz, SUBLANES, LANES), _tile_table(peer_v), s)
    return out.reshape(bsz, s, d)


def kernel(x, c, w_ada, b_ada, norm1_g, w_in, mu_rwkv, w0, w2_decay, a0, a2_iclr, g2_gate, k_k, k_a,
           r_k, lnx_w, lnx_b, q_norm_g, k_norm_g, w_branch, w_out, norm2_g, w_peer_q, peer_sub_keys,
           peer_u, peer_v):
    bsz, s, d = x.shape
    ada = _ada_call(c, w_ada, b_ada).reshape(bsz, 6, 1, d)
    shift1, scale1, gate1, shift2, scale2, gate2 = (ada[:, j] for j in range(6))
    w_r = _rwkv_weight_cols(w_in[:, 0:1824]).astype(BF16)
    w_q, w_k, w_v = (w_in[:, 1824 + j * WIDTH:1824 + (j + 1) * WIDTH].astype(BF16) for j in range(3))
    w_g = w_in[:, 1824 + 3 * WIDTH:].astype(BF16)
    p_r, q, k, v, gates = _proj_call(x, norm1_g, scale1, shift1, w_r, w_q, w_k, w_v, w_g)
    y_a = _rwkv_call(p_r, *_rwkv_params(mu_rwkv, w0, w2_decay, a0, a2_iclr, g2_gate, k_k, k_a, r_k,
                                        lnx_w, lnx_b))
    y_bt = _moba_branch(q, k, v, q_norm_g, k_norm_g)
    x1, h2, pq = _merge_call(x, y_a, y_bt, gates, w_branch[0].astype(BF16), w_branch[1].astype(BF16),
                             w_out.astype(BF16), gate1, norm2_g, scale2, shift2, w_peer_q.astype(BF16))
    return _peer_ffn_residual(x1, h2, pq, gate2, peer_sub_keys, peer_u, peer_v)
```

```python
import functools

import jax
import jax.numpy as jnp
from jax import lax
from jax.experimental import pallas as pl
from jax.experimental.pallas import tpu as pltpu

F32 = jnp.float32
BF16 = jnp.bfloat16
I32 = jnp.int32

LANES = 128
SUBLANES = 8
VMEM_LIMIT_BYTES = 56 * 1024 * 1024

NORM_EPS = 1e-6
LNX_EPS = 64e-5
NEG_INF = -1e30

HEAD_DIM = 64
N_HEADS = 8
WIDTH = N_HEADS * HEAD_DIM
RWKV_CHUNK = 64
RWKV_TILE = 128
RWKV_PCOLS = 2048
MOBA_BLOCK = 256
MOBA_TOPK = 3
PEER_HEADS = 8
PEER_KEYS = 128
PEER_TOPK = 16
PEER_PICKS = PEER_HEADS * PEER_TOPK


def _mm(a, b):
    return jnp.dot(a.astype(BF16), b.astype(BF16), preferred_element_type=F32)


def _mm_nt(a, b):
    return lax.dot_general(a.astype(BF16), b.astype(BF16), (((1,), (1,)), ((), ())),
                           preferred_element_type=F32)


def _mm_tn(a, b):
    return lax.dot_general(a.astype(BF16), b.astype(BF16), (((0,), (0,)), ((), ())),
                           preferred_element_type=F32)


def _mm_split(a, b):
    hi = a.astype(BF16)
    lo = (a - hi.astype(F32)).astype(BF16)
    b = b.astype(BF16)
    return (jnp.dot(hi, b, preferred_element_type=F32) + jnp.dot(lo, b, preferred_element_type=F32))


def _mmh(a, b):
    return jnp.dot(a, b, preferred_element_type=F32, precision=lax.Precision.HIGHEST)


def _head_ones(n):
    r = lax.broadcasted_iota(I32, (n, n), 0) // HEAD_DIM
    c = lax.broadcasted_iota(I32, (n, n), 1) // HEAD_DIM
    return (r == c).astype(F32)


def _sigmoid(x):
    return 1.0 / (1.0 + jnp.exp(-x))


def _ada_kernel(c_ref, w_ref, b_ref, o_ref):
    c = c_ref[...]
    o_ref[...] = _mmh(c * _sigmoid(c), w_ref[...]) + b_ref[...]


def _ada_call(c, w_ada, b_ada):
    bsz, d = c.shape
    n = w_ada.shape[1]
    tn = 1536
    return pl.pallas_call(
        _ada_kernel,
        grid=(n // tn,),
        in_specs=[pl.BlockSpec((bsz, d), lambda j: (0, 0)),
                  pl.BlockSpec((d, tn), lambda j: (0, j)),
                  pl.BlockSpec((1, tn), lambda j: (0, j))],
        out_specs=pl.BlockSpec((bsz, tn), lambda j: (0, j)),
        out_shape=jax.ShapeDtypeStruct((bsz, n), F32),
        compiler_params=pltpu.CompilerParams(dimension_semantics=("arbitrary",),
                                             vmem_limit_bytes=VMEM_LIMIT_BYTES),
        name="ada",
    )(c, w_ada, b_ada.reshape(1, n))


def _rwkv_kernel(p_ref, pprev_ref, mu_ref, vec_ref, w2_ref, a2_ref, g2_ref, y_ref, st_ref):
    i = pl.program_id(1)
    tile = p_ref.shape[1]
    L = RWKV_CHUNK

    @pl.when(i == 0)
    def _():
        st_ref[...] = jnp.zeros_like(st_ref)

    p = p_ref[0]
    prev_last = jnp.where(i == 0, 0.0, pprev_ref[0][SUBLANES - 1:SUBLANES, :])
    row = lax.broadcasted_iota(I32, p.shape, 0)
    p_prev = jnp.where(row == 0, prev_last, pltpu.roll(p, 1, 0))
    xm = p + (p_prev - p) * mu_ref[...]
    r = xm[:, 0:WIDTH]
    k = xm[:, WIDTH:2 * WIDTH]
    v = xm[:, 2 * WIDTH:3 * WIDTH]
    wl = xm[:, 3 * WIDTH:3 * WIDTH + 128]
    al = xm[:, 3 * WIDTH + 128:3 * WIDTH + 256]
    gl = xm[:, 3 * WIDTH + 256:3 * WIDTH + 512]
    w0, a0, k_k, k_a = vec_ref[0:1, :], vec_ref[1:2, :], vec_ref[2:3, :], vec_ref[3:4, :]
    r_k, lnx_w, lnx_b = vec_ref[4:5, :], vec_ref[5:6, :], vec_ref[6:7, :]

    z = -(w0 + _mmh(jnp.tanh(wl), w2_ref[...]))
    softplus = jnp.maximum(z, 0.0) + jnp.log(1.0 + jnp.exp(-jnp.abs(z)))
    lw = -jnp.exp(-softplus - 0.5)
    a = _sigmoid(a0 + _mmh(al, a2_ref[...]))
    g = _mmh(_sigmoid(gl), g2_ref[...])
    hones = _head_ones(WIDTH)
    kk = k * k_k
    kk = kk / jnp.maximum(jnp.sqrt(_mm(kk * kk, hones)), 1e-12)
    k2 = k * (1.0 + (a - 1.0) * k_a)
    a_s = -kk
    b_s = kk * a

    li = lax.broadcasted_iota(I32, (L, L), 0)
    lj = lax.broadcasted_iota(I32, (L, L), 1)
    tril_incl = li >= lj
    tril_strict = li > lj
    eye = (li == lj).astype(F32)
    csum = tril_incl.astype(F32)

    n_chunks = tile // L
    items = [(c, h) for c in range(n_chunks) for h in range(N_HEADS)]
    zeros_ll = jnp.zeros((L, L), F32)
    per_chunk = []
    for c in range(n_chunks):
        sl = slice(c * L, (c + 1) * L)
        lw_c = lw[sl]
        cum = _mmh(csum, lw_c)
        ref = cum[L // 2 - 1:L // 2, :]
        g_inv = jnp.exp(ref - cum)
        g_end = jnp.exp(cum[L - 1:L, :] - ref)
        bt = b_s[sl] * g_inv
        kt = k2[sl] * g_inv
        per_chunk.append(dict(
            at=a_s[sl] * jnp.exp(cum - lw_c - ref), rt=r[sl] * jnp.exp(cum - ref), bt=bt, kt=kt,
            bbar=bt * g_end, kbar=kt * g_end, v=v[sl], g_tot=jnp.exp(cum[L - 1:L, :]), g_ref=jnp.exp(ref)))

    def head(c, h, name):
        return per_chunk[c][name][:, h * HEAD_DIM:(h + 1) * HEAD_DIM]

    sc = [_mm_nt(jnp.concatenate([head(c, h, "at"), head(c, h, "rt")], axis=0),
                 jnp.concatenate([head(c, h, "bt"), head(c, h, "kt")], axis=0)) for c, h in items]
    m_ab = [jnp.where(tril_strict, s[0:L, 0:L], 0.0) for s in sc]
    m_ak = [jnp.where(tril_strict, s[0:L, L:2 * L], 0.0) for s in sc]
    tril_incl2 = (lax.broadcasted_iota(I32, (L, 2 * L), 0)
                  >= lax.broadcasted_iota(I32, (L, 2 * L), 1) % L)
    g_rbk = [jnp.where(tril_incl2, s[L:2 * L, :], 0.0) for s in sc]
    akv = [_mm(m, head(c, h, "v")) for m, (c, h) in zip(m_ak, items)]
    pw = [_mm(m, m) for m in m_ab]
    t_inv = [eye + m for m in m_ab]
    for _ in range(4):
        both = [_mm(jnp.concatenate([p, t], axis=0), p) for p, t in zip(pw, t_inv)]
        t_inv = [t + b[L:2 * L] for t, b in zip(t_inv, both)]
        pw = [b[0:L] for b in both]
    t_inv = [t + _mm(t, p) for t, p in zip(t_inv, pw)]
    tw = [_mm(t, jnp.concatenate([head(c, h, "at"), x], axis=1)) for t, x, (c, h) in zip(t_inv, akv, items)]
    ro = [_mm(g, jnp.concatenate([w, jnp.concatenate([zeros_ll, head(c, h, "v")], axis=1)], axis=0))
          for g, w, (c, h) in zip(g_rbk, tw, items)]
    a_hat = [w[:, 0:L] * head(c, h, "g_ref") for w, (c, h) in zip(tw, items)]
    r_hat = [(head(c, h, "rt") + x[:, 0:L]) * head(c, h, "g_ref") for x, (c, h) in zip(ro, items)]
    pp = [_mm_tn(jnp.concatenate([head(c, h, "bbar"), head(c, h, "kbar")], axis=0),
                 jnp.concatenate([jnp.concatenate([a, w[:, L:2 * L]], axis=1),
                                  jnp.concatenate([zeros_ll, head(c, h, "v")], axis=1)], axis=0))
          for a, w, (c, h) in zip(a_hat, tw, items)]
    state = [st_ref[h] for h in range(N_HEADS)]
    outs = []
    for c in range(n_chunks):
        head_outs = []
        for h in range(N_HEADS):
            i = c * N_HEADS + h
            phi_t = eye * head(c, h, "g_tot") + pp[i][:, 0:L]
            both = _mm(jnp.concatenate([r_hat[i], phi_t], axis=0), state[h])
            head_outs.append(both[0:L] + ro[i][:, L:2 * L])
            state[h] = both[L:2 * L] + pp[i][:, L:2 * L]
        outs.append(jnp.concatenate(head_outs, axis=1))
    for h in range(N_HEADS):
        st_ref[h] = state[h]
    out = jnp.concatenate(outs, axis=0)

    inv_n = 1.0 / HEAD_DIM
    mean = _mm(out, hones) * inv_n
    cen = out - mean
    var = _mm(cen * cen, hones) * inv_n
    out = cen * lax.rsqrt(var + LNX_EPS) * lnx_w + lnx_b
    bonus = _mm(r * k2 * r_k, hones) * v
    y_ref[0] = (out + bonus) * g


def _mmh_nt(a, b):
    return lax.dot_general(a, b, (((1,), (1,)), ((), ())), preferred_element_type=F32,
                           precision=lax.Precision.HIGHEST)


def _mmh_tn(a, b):
    return lax.dot_general(a, b, (((0,), (0,)), ((), ())), preferred_element_type=F32,
                           precision=lax.Precision.HIGHEST)


def _rwkv_call(p, mu, vecs, w2, a2, g2):
    bsz, s, pc = p.shape
    tile = RWKV_TILE
    const = lambda b, i: (0, 0)
    return pl.pallas_call(
        _rwkv_kernel,
        grid=(bsz, s // tile),
        in_specs=[pl.BlockSpec((1, tile, pc), lambda b, i: (b, i, 0)),
                  pl.BlockSpec((1, SUBLANES, pc),
                               lambda b, i: (b, jnp.maximum(i * (tile // SUBLANES) - 1, 0), 0)),
                  pl.BlockSpec((1, pc), const),
                  pl.BlockSpec((SUBLANES, WIDTH), const),
                  pl.BlockSpec((128, WIDTH), const),
                  pl.BlockSpec((128, WIDTH), const),
                  pl.BlockSpec((256, WIDTH), const)],
        out_specs=pl.BlockSpec((1, tile, WIDTH), lambda b, i: (b, i, 0)),
        out_shape=jax.ShapeDtypeStruct((bsz, s, WIDTH), F32),
        scratch_shapes=[pltpu.VMEM((N_HEADS, HEAD_DIM, HEAD_DIM), F32)],
        compiler_params=pltpu.CompilerParams(dimension_semantics=("arbitrary", "arbitrary"),
                                             vmem_limit_bytes=VMEM_LIMIT_BYTES),
        name="rwkv",
    )(p, p, mu, vecs, w2, a2, g2)


def _pad_rows(w, n):
    return jnp.pad(w, ((0, n - w.shape[0]), (0, 0)))


def _rwkv_params(mu_rwkv, w0, w2_decay, a0, a2_iclr, g2_gate, k_k, k_a, r_k, lnx_w, lnx_b):
    mu = jnp.concatenate([mu_rwkv[0:1536], _pad1(mu_rwkv[1536:1600], 128),
                          _pad1(mu_rwkv[1600:1664], 128), _pad1(mu_rwkv[1664:1824], 256)])
    vecs = jnp.stack([w0, a0, k_k, k_a, r_k.reshape(-1), lnx_w, lnx_b, jnp.zeros_like(w0)])
    return (mu.reshape(1, RWKV_PCOLS), vecs, _pad_rows(w2_decay, 128), _pad_rows(a2_iclr, 128),
            _pad_rows(g2_gate, 256))


def _pad1(v, n):
    return jnp.pad(v, (0, n - v.shape[0]))


def _pad_last(x, n):
    return jnp.pad(x, [(0, 0)] * (x.ndim - 1) + [(0, n - x.shape[-1])])


def _pad_cols(w, n):
    return _pad_last(w, n)


def _rwkv_weight_cols(w):
    return jnp.concatenate([w[:, 0:1536], _pad_cols(w[:, 1536:1600], 128),
                            _pad_cols(w[:, 1600:1664], 128), _pad_cols(w[:, 1664:1824], 256)], axis=1)


def _modulated_norm(x, gain, scale, shift):
    y = x * lax.rsqrt(jnp.mean(x * x, axis=-1, keepdims=True) + NORM_EPS)
    return y * gain * (1.0 + scale) + shift


def _proj_kernel(x_ref, g_ref, sc_ref, sh_ref, wr_ref, wq_ref, wk_ref, wv_ref, wg_ref,
                 pr_ref, q_ref, k_ref, v_ref, gate_ref):
    h = _modulated_norm(x_ref[0], g_ref[...], sc_ref[0], sh_ref[0]).astype(BF16)
    pr_ref[0] = jnp.dot(h, wr_ref[...], preferred_element_type=F32)
    q_ref[0] = jnp.dot(h, wq_ref[...], preferred_element_type=F32)
    k_ref[0] = jnp.dot(h, wk_ref[...], preferred_element_type=F32)
    v_ref[0] = jnp.dot(h, wv_ref[...], preferred_element_type=F32)
    gate_ref[0] = _sigmoid(jnp.dot(h, wg_ref[...], preferred_element_type=F32))


def _proj_call(x, norm_g, scale, shift, w_r, w_q, w_k, w_v, w_g):
    bsz, s, d = x.shape
    tm = 512
    tok = lambda b, i: (b, i, 0)
    per_b = lambda b, i: (b, 0, 0)
    const = lambda b, i: (0, 0)
    outs = [(RWKV_PCOLS, F32), (WIDTH, F32), (WIDTH, F32), (WIDTH, F32), (2 * d, F32)]
    return pl.pallas_call(
        _proj_kernel,
        grid=(bsz, s // tm),
        in_specs=[pl.BlockSpec((1, tm, d), tok),
                  pl.BlockSpec((1, d), const),
                  pl.BlockSpec((1, 1, d), per_b),
                  pl.BlockSpec((1, 1, d), per_b)]
                 + [pl.BlockSpec(w.shape, const) for w in (w_r, w_q, w_k, w_v, w_g)],
        out_specs=[pl.BlockSpec((1, tm, n), tok) for n, _ in outs],
        out_shape=[jax.ShapeDtypeStruct((bsz, s, n), dt) for n, dt in outs],
        compiler_params=pltpu.CompilerParams(dimension_semantics=("arbitrary", "arbitrary"),
                                             vmem_limit_bytes=VMEM_LIMIT_BYTES),
        name="proj",
    )(x, norm_g.reshape(1, d), scale, shift, w_r, w_q, w_k, w_v, w_g)


MOBA_PREP_TILE = 2 * MOBA_BLOCK


def _moba_prep_kernel(q_ref, k_ref, v_ref, qg_ref, kg_ref, qt_ref, kn_ref, vt_ref, sel_ref, kmean_ref):
    i = pl.program_id(1)
    tile = q_ref.shape[1]
    nb = kmean_ref.shape[0]
    blocks_per_tile = tile // MOBA_BLOCK

    @pl.when(i == 0)
    def _():
        kmean_ref[...] = jnp.zeros_like(kmean_ref)

    q, k, v = q_ref[0], k_ref[0], v_ref[0]
    hones = _head_ones(WIDTH)
    inv_n = 1.0 / HEAD_DIM
    qn = q * lax.rsqrt(_mm_split(q * q, hones) * inv_n + NORM_EPS) * qg_ref[...] * (HEAD_DIM ** -0.5)
    kn = k * lax.rsqrt(_mm_split(k * k, hones) * inv_n + NORM_EPS) * kg_ref[...]
    kn_ref[0] = kn.astype(BF16)

    kmean = kmean_ref[...]
    brow = lax.broadcasted_iota(I32, kmean.shape, 0)
    for blk in range(blocks_per_tile):
        km = jnp.mean(kn[blk * MOBA_BLOCK:(blk + 1) * MOBA_BLOCK], axis=0, keepdims=True)
        kmean = jnp.where(brow == i * blocks_per_tile + blk, km, kmean)
    kmean_ref[...] = kmean

    qnt = qn.T
    vt = v.T
    half = lax.broadcasted_iota(I32, (2 * HEAD_DIM, tile), 0) // HEAD_DIM
    n_iota = lax.broadcasted_iota(I32, (nb, tile), 0)
    q_blk = i * blocks_per_tile + lax.broadcasted_iota(I32, (nb, tile), 1) // MOBA_BLOCK
    for h in range(N_HEADS):
        pair = qnt[(h // 2) * 2 * HEAD_DIM:(h // 2 + 1) * 2 * HEAD_DIM]
        qt_ref[0, h] = jnp.where(half == h % 2, pair, 0.0).astype(BF16)
        for blk in range(blocks_per_tile):
            vt_ref[0, h, blk] = vt[h * HEAD_DIM:(h + 1) * HEAD_DIM,
                                   blk * MOBA_BLOCK:(blk + 1) * MOBA_BLOCK].astype(BF16)
        hs = slice(h * HEAD_DIM, (h + 1) * HEAD_DIM)
        gate = _mmh_nt(kmean[:, hs], qn[:, hs])
        work = jnp.where(n_iota < q_blk, gate, -jnp.inf)
        bits = jnp.zeros((1, tile), I32)
        for _ in range(MOBA_TOPK):
            m = jnp.max(work, axis=0, keepdims=True)
            idx = jnp.min(jnp.where(work == m, n_iota, nb), axis=0, keepdims=True)
            hit = n_iota == idx
            take = jnp.logical_and(hit, m > -jnp.inf)
            bits = bits | jnp.sum(jnp.where(take, jnp.left_shift(1, n_iota), 0), axis=0, keepdims=True)
            work = jnp.where(hit, -jnp.inf, work)
        sel_ref[0, h] = bits


def _moba_prep_call(q, k, v, q_norm_g, k_norm_g):
    bsz, s, w = q.shape
    tile = MOBA_PREP_TILE
    nb = s // MOBA_BLOCK
    assert nb <= 32, "block selection is a 32-bit mask"
    tok = lambda b, i: (b, i, 0)
    const = lambda b, i: (0, 0)
    return pl.pallas_call(
        _moba_prep_kernel,
        grid=(bsz, s // tile),
        in_specs=[pl.BlockSpec((1, tile, w), tok)] * 3 + [pl.BlockSpec((1, w), const)] * 2,
        out_specs=[pl.BlockSpec((1, N_HEADS, 2 * HEAD_DIM, tile), lambda b, i: (b, 0, 0, i)),
                   pl.BlockSpec((1, tile, w), tok),
                   pl.BlockSpec((1, N_HEADS, tile // MOBA_BLOCK, HEAD_DIM, MOBA_BLOCK),
                                lambda b, i: (b, 0, i, 0, 0)),
                   pl.BlockSpec((1, N_HEADS, 1, tile), lambda b, i: (b, 0, 0, i))],
        out_shape=[jax.ShapeDtypeStruct((bsz, N_HEADS, 2 * HEAD_DIM, s), BF16),
                   jax.ShapeDtypeStruct((bsz, s, w), BF16),
                   jax.ShapeDtypeStruct((bsz, N_HEADS, nb, HEAD_DIM, MOBA_BLOCK), BF16),
                   jax.ShapeDtypeStruct((bsz, N_HEADS, 1, s), I32)],
        scratch_shapes=[pltpu.VMEM((nb, w), F32)],
        compiler_params=pltpu.CompilerParams(dimension_semantics=("arbitrary", "arbitrary"),
                                             vmem_limit_bytes=VMEM_LIMIT_BYTES),
        name="moba_prep",
    )(q, k, v, jnp.tile(q_norm_g, N_HEADS).reshape(1, w), jnp.tile(k_norm_g, N_HEADS).reshape(1, w))


MOBA_KV_UNROLL = 4


def _moba_attn_kernel(qt_ref, k_ref, vt_ref, sel_ref, o_ref):
    qb = pl.program_id(2)
    bl = MOBA_BLOCK
    nb = vt_ref.shape[2]
    kpos = lax.broadcasted_iota(I32, (bl, bl), 0)
    qpos = lax.broadcasted_iota(I32, (bl, bl), 1)

    def key_block(j):
        return k_ref[0, pl.ds(pl.multiple_of(j * bl, bl), bl), :]

    k_own = key_block(qb)
    state = []
    for e in range(2):
        s = jnp.where(kpos <= qpos, jnp.dot(k_own, qt_ref[0, e], preferred_element_type=F32), NEG_INF)
        m = jnp.max(s, axis=0, keepdims=True)
        p = jnp.exp(s - m)
        l = jnp.sum(p, axis=0, keepdims=True)
        acc = jnp.dot(vt_ref[0, e, qb], p.astype(BF16), preferred_element_type=F32)
        state += [m, l, acc]

    def body(it, carry):
        carry = list(carry)
        js = [it * MOBA_KV_UNROLL + u for u in range(MOBA_KV_UNROLL)]
        jc = [jnp.minimum(j, nb - 1) for j in js]
        ks = [key_block(j) for j in jc]
        raw = [[jnp.dot(kj, qt_ref[0, e], preferred_element_type=F32) for kj in ks] for e in range(2)]
        probs = []
        for e in range(2):
            m, l, acc = carry[3 * e:3 * e + 3]
            sel = sel_ref[0, e]
            ss = []
            for j, j_in, s in zip(js, jc, raw[e]):
                picked = jnp.logical_and((jnp.right_shift(sel, j_in) & 1) != 0, j < qb)
                ss.append(jnp.where(picked, s, NEG_INF))
            m_new = m
            for s in ss:
                m_new = jnp.maximum(m_new, jnp.max(s, axis=0, keepdims=True))
            alpha = jnp.exp(m - m_new)
            ps = [jnp.exp(s - m_new) for s in ss]
            l = alpha * l
            for p in ps:
                l = l + jnp.sum(p, axis=0, keepdims=True)
            probs.append([p.astype(BF16) for p in ps])
            carry[3 * e:3 * e + 3] = [m_new, l, alpha * acc]
        for e in range(2):
            pv = [jnp.dot(vt_ref[0, e, j], p, preferred_element_type=F32) for j, p in zip(jc, probs[e])]
            carry[3 * e + 2] = carry[3 * e + 2] + sum(pv[1:], pv[0])
        return tuple(carry)

    trips = (qb + MOBA_KV_UNROLL - 1) // MOBA_KV_UNROLL
    state = lax.fori_loop(0, trips, body, tuple(state))
    for e in range(2):
        o_ref[0, e] = state[3 * e + 2] / state[3 * e + 1]


def _moba_attn_call(qt, kn, vt, sel):
    bsz, nh, _, s = qt.shape
    nb = s // MOBA_BLOCK
    return pl.pallas_call(
        _moba_attn_kernel,
        grid=(bsz, nh // 2, nb),
        in_specs=[pl.BlockSpec((1, 2, 2 * HEAD_DIM, MOBA_BLOCK), lambda b, h, i: (b, h, 0, i)),
                  pl.BlockSpec((1, s, 2 * HEAD_DIM), lambda b, h, i: (b, 0, h)),
                  pl.BlockSpec((1, 2, nb, HEAD_DIM, MOBA_BLOCK), lambda b, h, i: (b, h, 0, 0, 0)),
                  pl.BlockSpec((1, 2, 1, MOBA_BLOCK), lambda b, h, i: (b, h, 0, i))],
        out_specs=pl.BlockSpec((1, 2, HEAD_DIM, MOBA_BLOCK), lambda b, h, i: (b, h, 0, i)),
        out_shape=jax.ShapeDtypeStruct((bsz, nh, HEAD_DIM, s), F32),
        compiler_params=pltpu.CompilerParams(
            dimension_semantics=("arbitrary", "arbitrary", "arbitrary"),
            vmem_limit_bytes=VMEM_LIMIT_BYTES),
        name="moba_attn",
    )(qt, kn, vt, sel)


def _moba_branch(q, k, v, q_norm_g, k_norm_g):
    qt, kn, vt, sel = _moba_prep_call(q, k, v, q_norm_g, k_norm_g)
    yt = _moba_attn_call(qt, kn, vt, sel)
    return yt.reshape(yt.shape[0], WIDTH, yt.shape[3])


def _merge_kernel(x_ref, ya_ref, ybt_ref, gates_ref, wb0_ref, wb1_ref, wo_ref, g1_ref, n2_ref, sc2_ref,
                  sh2_ref, wq_ref, x1_ref, h2_ref, q_ref):
    d = x_ref.shape[2]
    ys_a = _mm(ya_ref[0], wb0_ref[...])
    ys_b = _mm_tn(ybt_ref[0], wb1_ref[...])
    gates = gates_ref[0]
    mixed = gates[:, 0:d] * ys_a + gates[:, d:2 * d] * ys_b
    x1 = x_ref[0] + g1_ref[0] * _mm(mixed, wo_ref[...])
    x1_ref[0] = x1
    h2 = _modulated_norm(x1, n2_ref[...], sc2_ref[0], sh2_ref[0])
    h2_ref[0] = h2
    q_ref[0] = _mm(h2, wq_ref[...])


def _merge_call(x, y_a, y_bt, gates, w_b0, w_b1, w_out, gate1, norm2_g, scale2, shift2, w_peer_q):
    bsz, s, d = x.shape
    tm = 256
    nq = w_peer_q.shape[1]
    tok = lambda b, i: (b, i, 0)
    per_b = lambda b, i: (b, 0, 0)
    const = lambda b, i: (0, 0)
    return pl.pallas_call(
        _merge_kernel,
        grid=(bsz, s // tm),
        in_specs=[pl.BlockSpec((1, tm, d), tok),
                  pl.BlockSpec((1, tm, WIDTH), tok),
                  pl.BlockSpec((1, WIDTH, tm), lambda b, i: (b, 0, i)),
                  pl.BlockSpec((1, tm, 2 * d), tok),
                  pl.BlockSpec((WIDTH, d), const),
                  pl.BlockSpec((WIDTH, d), const),
                  pl.BlockSpec((d, d), const),
                  pl.BlockSpec((1, 1, d), per_b),
                  pl.BlockSpec((1, d), const),
                  pl.BlockSpec((1, 1, d), per_b),
                  pl.BlockSpec((1, 1, d), per_b),
                  pl.BlockSpec((d, nq), const)],
        out_specs=[pl.BlockSpec((1, tm, d), tok), pl.BlockSpec((1, tm, d), tok),
                   pl.BlockSpec((1, tm, nq), tok)],
        out_shape=[jax.ShapeDtypeStruct((bsz, s, d), F32), jax.ShapeDtypeStruct((bsz, s, d), F32),
                   jax.ShapeDtypeStruct((bsz, s, nq), F32)],
        compiler_params=pltpu.CompilerParams(dimension_semantics=("arbitrary", "arbitrary"),
                                             vmem_limit_bytes=VMEM_LIMIT_BYTES),
        name="merge",
    )(x, y_a, y_bt, gates, w_b0, w_b1, w_out, gate1, norm2_g.reshape(1, d), scale2, shift2, w_peer_q)


def _topk_rows(x, k, payload=None):
    n = x.shape[0]
    iota = lax.broadcasted_iota(I32, x.shape, 0)
    vals, picks = [], []
    work = x
    for _ in range(k):
        m = jnp.max(work, axis=0, keepdims=True)
        idx = jnp.min(jnp.where(work == m, iota, n), axis=0, keepdims=True)
        hit = iota == idx
        vals.append(m)
        picks.append(idx if payload is None else jnp.max(jnp.where(hit, payload, -1), axis=0, keepdims=True))
        work = jnp.where(hit, -jnp.inf, work)
    return jnp.concatenate(vals, axis=0), jnp.concatenate(picks, axis=0)


def _route_kernel(q_ref, keys_ref, code_ref, gw_ref):
    k = PEER_TOPK
    eids, gws = [], []
    for hp in range(PEER_HEADS):
        q1 = q_ref[:, (2 * hp) * PEER_KEYS:(2 * hp + 1) * PEER_KEYS]
        q2 = q_ref[:, (2 * hp + 1) * PEER_KEYS:(2 * hp + 2) * PEER_KEYS]
        s1 = _mm_nt(keys_ref[0, hp], q1)
        s2 = _mm_nt(keys_ref[1, hp], q2)
        v1, i1 = _topk_rows(s1, k)
        v2, i2 = _topk_rows(s2, k)
        widths = [k // (a + 1) for a in range(k)]
        pad = -sum(widths) % SUBLANES
        comb = jnp.concatenate([v1[a:a + 1] + v2[0:w] for a, w in enumerate(widths)]
                               + [jnp.full((pad, v1.shape[1]), -jnp.inf, F32)], axis=0)
        expert = jnp.concatenate([i1[a:a + 1] * PEER_KEYS + i2[0:w] for a, w in enumerate(widths)]
                                 + [jnp.zeros((pad, v1.shape[1]), I32)], axis=0)
        sc, eid = _topk_rows(comb, k, payload=expert)
        e = jnp.exp(sc - sc[0:1])
        gws.append(e / jnp.sum(e, axis=0, keepdims=True))
        eids.append(eid)
    code_ref[...] = jnp.concatenate(eids, axis=0).T
    gw_ref[...] = jnp.concatenate(gws, axis=0).T


def _route_call(q, sub_keys):
    t, nq = q.shape
    tt = 256
    return pl.pallas_call(
        _route_kernel,
        grid=(t // tt,),
        in_specs=[pl.BlockSpec((tt, nq), lambda i: (i, 0)),
                  pl.BlockSpec(sub_keys.shape, lambda i: (0, 0, 0, 0))],
        out_specs=[pl.BlockSpec((tt, PEER_PICKS), lambda i: (i, 0)),
                   pl.BlockSpec((tt, PEER_PICKS), lambda i: (i, 0))],
        out_shape=[jax.ShapeDtypeStruct((t, PEER_PICKS), I32), jax.ShapeDtypeStruct((t, PEER_PICKS), F32)],
        compiler_params=pltpu.CompilerParams(dimension_semantics=("arbitrary",),
                                             vmem_limit_bytes=VMEM_LIMIT_BYTES),
        name="route",
    )(q, sub_keys)


PEER_TOKENS_PER_STEP = 64
PEER_ACCUMULATORS = 4


def _tile_table(tab):
    n, d = tab.shape
    return tab.astype(BF16).reshape(n, d // LANES, LANES)


def _gelu(x):
    return 0.5 * x * (1.0 + lax.erf(x * (2.0 ** -0.5)))


def _sublane_sums(tiles):
    row = lax.broadcasted_iota(I32, (SUBLANES, LANES), 0)
    odd, upper_pair, upper_half = (row & 1) != 0, (row & 2) != 0, row >= 4

    def quad(a, b, c, d):
        ab = jnp.where(odd, a + pltpu.roll(a, 1, 0), b + pltpu.roll(b, SUBLANES - 1, 0))
        cd = jnp.where(odd, c + pltpu.roll(c, 1, 0), d + pltpu.roll(d, SUBLANES - 1, 0))
        q = jnp.where(upper_pair, ab + pltpu.roll(ab, 2, 0), cd + pltpu.roll(cd, SUBLANES - 2, 0))
        return q + pltpu.roll(q, 4, 0)

    return jnp.where(upper_half, quad(tiles[7], tiles[6], tiles[5], tiles[4]),
                     quad(tiles[3], tiles[2], tiles[1], tiles[0]))


def _peer_act_kernel(code_ref, h_ref, gw_ref, tab_ref, act_ref, psum_ref):
    tokens = h_ref.shape[0]

    def token(t, carry):
        h = h_ref[t]
        base = pl.multiple_of(t * PEER_PICKS, PEER_PICKS)
        for j0 in range(0, PEER_PICKS, SUBLANES):
            prods = [tab_ref[code_ref[base + j0 + j]].astype(F32) * h for j in range(SUBLANES)]
            psum_ref[pl.ds(base + j0, SUBLANES), :] = _sublane_sums(prods)
        return carry

    lax.fori_loop(0, tokens, token, 0)
    ones = jnp.ones((SUBLANES, LANES), BF16)
    dots = jnp.concatenate(
        [_mm_nt(ones, psum_ref[t * PEER_PICKS:(t + 1) * PEER_PICKS, :])[0:1] for t in range(tokens)], axis=0)
    act_ref[...] = _gelu(dots) * gw_ref[...]


def _peer_act_call(code, h2, gw, tab):
    t = h2.shape[0]
    tt = PEER_TOKENS_PER_STEP
    return pl.pallas_call(
        _peer_act_kernel,
        grid=(t // tt,),
        in_specs=[pl.BlockSpec((tt * PEER_PICKS,), lambda i: (i,), memory_space=pltpu.SMEM),
                  pl.BlockSpec((tt, SUBLANES, LANES), lambda i: (i, 0, 0)),
                  pl.BlockSpec((tt, PEER_PICKS), lambda i: (i, 0)),
                  pl.BlockSpec(tab.shape, lambda i: (0, 0, 0), pipeline_mode=pl.Buffered(1))],
        out_specs=pl.BlockSpec((tt, PEER_PICKS), lambda i: (i, 0)),
        out_shape=jax.ShapeDtypeStruct((t, PEER_PICKS), F32),
        scratch_shapes=[pltpu.VMEM((tt * PEER_PICKS, LANES), F32)],
        compiler_params=pltpu.CompilerParams(dimension_semantics=("arbitrary",),
                                             vmem_limit_bytes=VMEM_LIMIT_BYTES),
        name="peer_act",
    )(code, h2, gw, tab)


def _peer_out_kernel(code_ref, act_ref, x_ref, g2_ref, tab_ref, o_ref, wrep_ref):
    tokens = x_ref.shape[0]
    g2 = g2_ref[0]
    eye = (lax.broadcasted_iota(I32, (PEER_PICKS, PEER_PICKS), 0)
           == lax.broadcasted_iota(I32, (PEER_PICKS, PEER_PICKS), 1)).astype(F32)
    ones = jnp.ones((PEER_PICKS, LANES), BF16)
    for t in range(tokens):
        wrep_ref[t * PEER_PICKS:(t + 1) * PEER_PICKS, :] = _mm_split(eye * act_ref[t:t + 1, :], ones)

    def token(t, carry):
        accs = [jnp.zeros((SUBLANES, LANES), F32) for _ in range(PEER_ACCUMULATORS)]
        for s in range(PEER_PICKS):
            w = jnp.broadcast_to(wrep_ref[pl.ds(t * PEER_PICKS + s, 1), :], (SUBLANES, LANES))
            accs[s % PEER_ACCUMULATORS] += w * tab_ref[code_ref[t * PEER_PICKS + s]].astype(F32)
        o_ref[t] = x_ref[t] + g2 * ((accs[0] + accs[1]) + (accs[2] + accs[3]))
        return carry

    lax.fori_loop(0, tokens, token, 0)


def _peer_out_call(code, act, x1, gate2, tab, tokens_per_batch):
    t = x1.shape[0]
    tt = PEER_TOKENS_PER_STEP
    steps_per_batch = tokens_per_batch // tt
    return pl.pallas_call(
        _peer_out_kernel,
        grid=(t // tt,),
        in_specs=[pl.BlockSpec((tt * PEER_PICKS,), lambda i: (i,), memory_space=pltpu.SMEM),
                  pl.BlockSpec((tt, PEER_PICKS), lambda i: (i, 0)),
                  pl.BlockSpec((tt, SUBLANES, LANES), lambda i: (i, 0, 0)),
                  pl.BlockSpec((1, SUBLANES, LANES), lambda i: (i // steps_per_batch, 0, 0)),
                  pl.BlockSpec(tab.shape, lambda i: (0, 0, 0), pipeline_mode=pl.Buffered(1))],
        out_specs=pl.BlockSpec((tt, SUBLANES, LANES), lambda i: (i, 0, 0)),
        out_shape=jax.ShapeDtypeStruct((t, SUBLANES, LANES), F32),
        scratch_shapes=[pltpu.VMEM((tt * PEER_PICKS, LANES), F32)],
        compiler_params=pltpu.CompilerParams(dimension_semantics=("arbitrary",),
                                             vmem_limit_bytes=VMEM_LIMIT_BYTES),
        name="peer_out",
    )(code, act, x1, gate2, tab)


def _peer_ffn_residual(x1, h2, q, gate2, sub_keys, peer_u, peer_v):
    bsz, s, d = x1.shape
    t = bsz * s
    code, gw = _route_call(q.reshape(t, -1), sub_keys.astype(BF16))
    code = code.reshape(-1)
    act = _peer_act_call(code, h2.reshape(t, SUBLANES, LANES), gw, _tile_table(peer_u))
    out = _peer_out_call(code, act, x1.reshape(t, SUBLANES, LANES),
                         gate2.reshape(bsz, SUBLANES, LANES), _tile_table(peer_v), s)
    return out.reshape(bsz, s, d)


def kernel(x, c, w_ada, b_ada, norm1_g, w_in, mu_rwkv, w0, w2_decay, a0, a2_iclr, g2_gate, k_k, k_a,
           r_k, lnx_w, lnx_b, q_norm_g, k_norm_g, w_branch, w_out, norm2_g, w_peer_q, peer_sub_keys,
           peer_u, peer_v):
    bsz, s, d = x.shape
    ada = _ada_call(c, w_ada, b_ada).reshape(bsz, 6, 1, d)
    shift1, scale1, gate1, shift2, scale2, gate2 = (ada[:, j] for j in range(6))
    w_r = _rwkv_weight_cols(w_in[:, 0:1824]).astype(BF16)
    w_q, w_k, w_v = (w_in[:, 1824 + j * WIDTH:1824 + (j + 1) * WIDTH].astype(BF16) for j in range(3))
    w_g = w_in[:, 1824 + 3 * WIDTH:].astype(BF16)
    p_r, q, k, v, gates = _proj_call(x, norm1_g, scale1, shift1, w_r, w_q, w_k, w_v, w_g)
    y_a = _rwkv_call(p_r, *_rwkv_params(mu_rwkv, w0, w2_decay, a0, a2_iclr, g2_gate, k_k, k_a, r_k,
                                        lnx_w, lnx_b))
    y_bt = _moba_branch(q, k, v, q_norm_g, k_norm_g)
    x1, h2, pq = _merge_call(x, y_a, y_bt, gates, w_branch[0].astype(BF16), w_branch[1].astype(BF16),
                             w_out.astype(BF16), gate1, norm2_g, scale2, shift2, w_peer_q.astype(BF16))
    return _peer_ffn_residual(x1, h2, pq, gate2, peer_sub_keys, peer_u, peer_v)
```

```python
import functools

import jax
import jax.numpy as jnp
from jax import lax
from jax.experimental import pallas as pl
from jax.experimental.pallas import tpu as pltpu

F32 = jnp.float32
BF16 = jnp.bfloat16
I32 = jnp.int32

LANES = 128
SUBLANES = 8
VMEM_LIMIT_BYTES = 56 * 1024 * 1024

NORM_EPS = 1e-6
LNX_EPS = 64e-5
NEG_INF = -1e30

HEAD_DIM = 64
N_HEADS = 8
WIDTH = N_HEADS * HEAD_DIM
RWKV_CHUNK = 64
RWKV_TILE = 128
RWKV_PCOLS = 2048
MOBA_BLOCK = 256
MOBA_TOPK = 3
PEER_HEADS = 8
PEER_KEYS = 128
PEER_TOPK = 16
PEER_PICKS = PEER_HEADS * PEER_TOPK


def _mm(a, b):
    return jnp.dot(a.astype(BF16), b.astype(BF16), preferred_element_type=F32)


def _mm_nt(a, b):
    return lax.dot_general(a.astype(BF16), b.astype(BF16), (((1,), (1,)), ((), ())),
                           preferred_element_type=F32)


def _mm_tn(a, b):
    return lax.dot_general(a.astype(BF16), b.astype(BF16), (((0,), (0,)), ((), ())),
                           preferred_element_type=F32)


def _mm_split(a, b):
    hi = a.astype(BF16)
    lo = (a - hi.astype(F32)).astype(BF16)
    b = b.astype(BF16)
    return (jnp.dot(hi, b, preferred_element_type=F32) + jnp.dot(lo, b, preferred_element_type=F32))


def _mmh(a, b):
    return jnp.dot(a, b, preferred_element_type=F32, precision=lax.Precision.HIGHEST)


def _head_ones(n):
    r = lax.broadcasted_iota(I32, (n, n), 0) // HEAD_DIM
    c = lax.broadcasted_iota(I32, (n, n), 1) // HEAD_DIM
    return (r == c).astype(F32)


def _sigmoid(x):
    return 1.0 / (1.0 + jnp.exp(-x))


def _ada_kernel(c_ref, w_ref, b_ref, o_ref):
    c = c_ref[...]
    o_ref[...] = _mmh(c * _sigmoid(c), w_ref[...]) + b_ref[...]


def _ada_call(c, w_ada, b_ada):
    bsz, d = c.shape
    n = w_ada.shape[1]
    tn = 1536
    return pl.pallas_call(
        _ada_kernel,
        grid=(n // tn,),
        in_specs=[pl.BlockSpec((bsz, d), lambda j: (0, 0)),
                  pl.BlockSpec((d, tn), lambda j: (0, j)),
                  pl.BlockSpec((1, tn), lambda j: (0, j))],
        out_specs=pl.BlockSpec((bsz, tn), lambda j: (0, j)),
        out_shape=jax.ShapeDtypeStruct((bsz, n), F32),
        compiler_params=pltpu.CompilerParams(dimension_semantics=("arbitrary",),
                                             vmem_limit_bytes=VMEM_LIMIT_BYTES),
        name="ada",
    )(c, w_ada, b_ada.reshape(1, n))


def _rwkv_kernel(p_ref, pprev_ref, mu_ref, vec_ref, w2_ref, a2_ref, g2_ref, y_ref, st_ref):
    i = pl.program_id(1)
    tile = p_ref.shape[1]
    L = RWKV_CHUNK

    @pl.when(i == 0)
    def _():
        st_ref[...] = jnp.zeros_like(st_ref)

    p = p_ref[0]
    prev_last = jnp.where(i == 0, 0.0, pprev_ref[0][SUBLANES - 1:SUBLANES, :])
    row = lax.broadcasted_iota(I32, p.shape, 0)
    p_prev = jnp.where(row == 0, prev_last, pltpu.roll(p, 1, 0))
    xm = p + (p_prev - p) * mu_ref[...]
    r = xm[:, 0:WIDTH]
    k = xm[:, WIDTH:2 * WIDTH]
    v = xm[:, 2 * WIDTH:3 * WIDTH]
    wl = xm[:, 3 * WIDTH:3 * WIDTH + 128]
    al = xm[:, 3 * WIDTH + 128:3 * WIDTH + 256]
    gl = xm[:, 3 * WIDTH + 256:3 * WIDTH + 512]
    w0, a0, k_k, k_a = vec_ref[0:1, :], vec_ref[1:2, :], vec_ref[2:3, :], vec_ref[3:4, :]
    r_k, lnx_w, lnx_b = vec_ref[4:5, :], vec_ref[5:6, :], vec_ref[6:7, :]

    z = -(w0 + _mmh(jnp.tanh(wl), w2_ref[...]))
    softplus = jnp.maximum(z, 0.0) + jnp.log(1.0 + jnp.exp(-jnp.abs(z)))
    lw = -jnp.exp(-softplus - 0.5)
    a = _sigmoid(a0 + _mmh(al, a2_ref[...]))
    g = _mmh(_sigmoid(gl), g2_ref[...])
    hones = _head_ones(WIDTH)
    kk = k * k_k
    kk = kk / jnp.maximum(jnp.sqrt(_mm(kk * kk, hones)), 1e-12)
    k2 = k * (1.0 + (a - 1.0) * k_a)
    a_s = -kk
    b_s = kk * a

    li = lax.broadcasted_iota(I32, (L, L), 0)
    lj = lax.broadcasted_iota(I32, (L, L), 1)
    tril_incl = li >= lj
    tril_strict = li > lj
    eye = (li == lj).astype(F32)
    csum = tril_incl.astype(F32)

    n_chunks = tile // L
    items = [(c, h) for c in range(n_chunks) for h in range(N_HEADS)]
    zeros_ll = jnp.zeros((L, L), F32)
    per_chunk = []
    for c in range(n_chunks):
        sl = slice(c * L, (c + 1) * L)
        lw_c = lw[sl]
        cum = _mmh(csum, lw_c)
        ref = cum[L // 2 - 1:L // 2, :]
        g_inv = jnp.exp(ref - cum)
        g_end = jnp.exp(cum[L - 1:L, :] - ref)
        bt = b_s[sl] * g_inv
        kt = k2[sl] * g_inv
        per_chunk.append(dict(
            at=a_s[sl] * jnp.exp(cum - lw_c - ref), rt=r[sl] * jnp.exp(cum - ref), bt=bt, kt=kt,
            bbar=bt * g_end, kbar=kt * g_end, v=v[sl], g_tot=jnp.exp(cum[L - 1:L, :]), g_ref=jnp.exp(ref)))

    def head(c, h, name):
        return per_chunk[c][name][:, h * HEAD_DIM:(h + 1) * HEAD_DIM]

    sc = [_mm_nt(jnp.concatenate([head(c, h, "at"), head(c, h, "rt")], axis=0),
                 jnp.concatenate([head(c, h, "bt"), head(c, h, "kt")], axis=0)) for c, h in items]
    m_ab = [jnp.where(tril_strict, s[0:L, 0:L], 0.0) for s in sc]
    m_ak = [jnp.where(tril_strict, s[0:L, L:2 * L], 0.0) for s in sc]
    tril_incl2 = (lax.broadcasted_iota(I32, (L, 2 * L), 0)
                  >= lax.broadcasted_iota(I32, (L, 2 * L), 1) % L)
    g_rbk = [jnp.where(tril_incl2, s[L:2 * L, :], 0.0) for s in sc]
    akv = [_mm(m, head(c, h, "v")) for m, (c, h) in zip(m_ak, items)]
    pw = [_mm(m, m) for m in m_ab]
    t_inv = [eye + m for m in m_ab]
    for _ in range(4):
        both = [_mm(jnp.concatenate([p, t], axis=0), p) for p, t in zip(pw, t_inv)]
        t_inv = [t + b[L:2 * L] for t, b in zip(t_inv, both)]
        pw = [b[0:L] for b in both]
    t_inv = [t + _mm(t, p) for t, p in zip(t_inv, pw)]
    tw = [_mm(t, jnp.concatenate([head(c, h, "at"), x], axis=1)) for t, x, (c, h) in zip(t_inv, akv, items)]
    ro = [_mm(g, jnp.concatenate([w, jnp.concatenate([zeros_ll, head(c, h, "v")], axis=1)], axis=0))
          for g, w, (c, h) in zip(g_rbk, tw, items)]
    a_hat = [w[:, 0:L] * head(c, h, "g_ref") for w, (c, h) in zip(tw, items)]
    r_hat = [(head(c, h, "rt") + x[:, 0:L]) * head(c, h, "g_ref") for x, (c, h) in zip(ro, items)]
    pp = [_mm_tn(jnp.concatenate([head(c, h, "bbar"), head(c, h, "kbar")], axis=0),
                 jnp.concatenate([jnp.concatenate([a, w[:, L:2 * L]], axis=1),
                                  jnp.concatenate([zeros_ll, head(c, h, "v")], axis=1)], axis=0))
          for a, w, (c, h) in zip(a_hat, tw, items)]
    state = [st_ref[h] for h in range(N_HEADS)]
    outs = []
    for c in range(n_chunks):
        head_outs = []
        for h in range(N_HEADS):
            i = c * N_HEADS + h
            phi_t = eye * head(c, h, "g_tot") + pp[i][:, 0:L]
            both = _mm(jnp.concatenate([r_hat[i], phi_t], axis=0), state[h])
            head_outs.append(both[0:L] + ro[i][:, L:2 * L])
            state[h] = both[L:2 * L] + pp[i][:, L:2 * L]
        outs.append(jnp.concatenate(head_outs, axis=1))
    for h in range(N_HEADS):
        st_ref[h] = state[h]
    out = jnp.concatenate(outs, axis=0)

    inv_n = 1.0 / HEAD_DIM
    mean = _mm(out, hones) * inv_n
    cen = out - mean
    var = _mm(cen * cen, hones) * inv_n
    out = cen * lax.rsqrt(var + LNX_EPS) * lnx_w + lnx_b
    bonus = _mm(r * k2 * r_k, hones) * v
    y_ref[0] = (out + bonus) * g


def _mmh_nt(a, b):
    return lax.dot_general(a, b, (((1,), (1,)), ((), ())), preferred_element_type=F32,
                           precision=lax.Precision.HIGHEST)


def _mmh_tn(a, b):
    return lax.dot_general(a, b, (((0,), (0,)), ((), ())), preferred_element_type=F32,
                           precision=lax.Precision.HIGHEST)


def _rwkv_call(p, mu, vecs, w2, a2, g2):
    bsz, s, pc = p.shape
    tile = RWKV_TILE
    const = lambda b, i: (0, 0)
    return pl.pallas_call(
        _rwkv_kernel,
        grid=(bsz, s // tile),
        in_specs=[pl.BlockSpec((1, tile, pc), lambda b, i: (b, i, 0)),
                  pl.BlockSpec((1, SUBLANES, pc),
                               lambda b, i: (b, jnp.maximum(i * (tile // SUBLANES) - 1, 0), 0)),
                  pl.BlockSpec((1, pc), const),
                  pl.BlockSpec((SUBLANES, WIDTH), const),
                  pl.BlockSpec((128, WIDTH), const),
                  pl.BlockSpec((128, WIDTH), const),
                  pl.BlockSpec((256, WIDTH), const)],
        out_specs=pl.BlockSpec((1, tile, WIDTH), lambda b, i: (b, i, 0)),
        out_shape=jax.ShapeDtypeStruct((bsz, s, WIDTH), F32),
        scratch_shapes=[pltpu.VMEM((N_HEADS, HEAD_DIM, HEAD_DIM), F32)],
        compiler_params=pltpu.CompilerParams(dimension_semantics=("arbitrary", "arbitrary"),
                                             vmem_limit_bytes=VMEM_LIMIT_BYTES),
        name="rwkv",
    )(p, p, mu, vecs, w2, a2, g2)


def _pad_rows(w, n):
    return jnp.pad(w, ((0, n - w.shape[0]), (0, 0)))


def _rwkv_params(mu_rwkv, w0, w2_decay, a0, a2_iclr, g2_gate, k_k, k_a, r_k, lnx_w, lnx_b):
    mu = jnp.concatenate([mu_rwkv[0:1536], _pad1(mu_rwkv[1536:1600], 128),
                          _pad1(mu_rwkv[1600:1664], 128), _pad1(mu_rwkv[1664:1824], 256)])
    vecs = jnp.stack([w0, a0, k_k, k_a, r_k.reshape(-1), lnx_w, lnx_b, jnp.zeros_like(w0)])
    return (mu.reshape(1, RWKV_PCOLS), vecs, _pad_rows(w2_decay, 128), _pad_rows(a2_iclr, 128),
            _pad_rows(g2_gate, 256))


def _pad1(v, n):
    return jnp.pad(v, (0, n - v.shape[0]))


def _pad_last(x, n):
    return jnp.pad(x, [(0, 0)] * (x.ndim - 1) + [(0, n - x.shape[-1])])


def _pad_cols(w, n):
    return _pad_last(w, n)


def _rwkv_weight_cols(w):
    return jnp.concatenate([w[:, 0:1536], _pad_cols(w[:, 1536:1600], 128),
                            _pad_cols(w[:, 1600:1664], 128), _pad_cols(w[:, 1664:1824], 256)], axis=1)


def _modulated_norm(x, gain, scale, shift):
    y = x * lax.rsqrt(jnp.mean(x * x, axis=-1, keepdims=True) + NORM_EPS)
    return y * gain * (1.0 + scale) + shift


def _proj_kernel(x_ref, g_ref, sc_ref, sh_ref, wr_ref, wq_ref, wk_ref, wv_ref, wg_ref,
                 pr_ref, q_ref, k_ref, v_ref, gate_ref):
    h = _modulated_norm(x_ref[0], g_ref[...], sc_ref[0], sh_ref[0]).astype(BF16)
    pr_ref[0] = jnp.dot(h, wr_ref[...], preferred_element_type=F32)
    q_ref[0] = jnp.dot(h, wq_ref[...], preferred_element_type=F32)
    k_ref[0] = jnp.dot(h, wk_ref[...], preferred_element_type=F32)
    v_ref[0] = jnp.dot(h, wv_ref[...], preferred_element_type=F32)
    gate_ref[0] = _sigmoid(jnp.dot(h, wg_ref[...], preferred_element_type=F32))


def _proj_call(x, norm_g, scale, shift, w_r, w_q, w_k, w_v, w_g):
    bsz, s, d = x.shape
    tm = 512
    tok = lambda b, i: (b, i, 0)
    per_b = lambda b, i: (b, 0, 0)
    const = lambda b, i: (0, 0)
    outs = [(RWKV_PCOLS, F32), (WIDTH, F32), (WIDTH, F32), (WIDTH, F32), (2 * d, F32)]
    return pl.pallas_call(
        _proj_kernel,
        grid=(bsz, s // tm),
        in_specs=[pl.BlockSpec((1, tm, d), tok),
                  pl.BlockSpec((1, d), const),
                  pl.BlockSpec((1, 1, d), per_b),
                  pl.BlockSpec((1, 1, d), per_b)]
                 + [pl.BlockSpec(w.shape, const) for w in (w_r, w_q, w_k, w_v, w_g)],
        out_specs=[pl.BlockSpec((1, tm, n), tok) for n, _ in outs],
        out_shape=[jax.ShapeDtypeStruct((bsz, s, n), dt) for n, dt in outs],
        compiler_params=pltpu.CompilerParams(dimension_semantics=("arbitrary", "arbitrary"),
                                             vmem_limit_bytes=VMEM_LIMIT_BYTES),
        name="proj",
    )(x, norm_g.reshape(1, d), scale, shift, w_r, w_q, w_k, w_v, w_g)


MOBA_PREP_TILE = 2 * MOBA_BLOCK


def _moba_prep_kernel(q_ref, k_ref, v_ref, qg_ref, kg_ref, qt_ref, kn_ref, vt_ref, sel_ref, kmean_ref):
    i = pl.program_id(1)
    tile = q_ref.shape[1]
    nb = kmean_ref.shape[0]
    blocks_per_tile = tile // MOBA_BLOCK

    @pl.when(i == 0)
    def _():
        kmean_ref[...] = jnp.zeros_like(kmean_ref)

    q, k, v = q_ref[0], k_ref[0], v_ref[0]
    hones = _head_ones(WIDTH)
    inv_n = 1.0 / HEAD_DIM
    qn = q * lax.rsqrt(_mm_split(q * q, hones) * inv_n + NORM_EPS) * qg_ref[...] * (HEAD_DIM ** -0.5)
    kn = k * lax.rsqrt(_mm_split(k * k, hones) * inv_n + NORM_EPS) * kg_ref[...]
    kn_ref[0] = kn.astype(BF16)

    kmean = kmean_ref[...]
    brow = lax.broadcasted_iota(I32, kmean.shape, 0)
    for blk in range(blocks_per_tile):
        km = jnp.mean(kn[blk * MOBA_BLOCK:(blk + 1) * MOBA_BLOCK], axis=0, keepdims=True)
        kmean = jnp.where(brow == i * blocks_per_tile + blk, km, kmean)
    kmean_ref[...] = kmean

    qnt = qn.T
    vt = v.T
    half = lax.broadcasted_iota(I32, (2 * HEAD_DIM, tile), 0) // HEAD_DIM
    n_iota = lax.broadcasted_iota(I32, (nb, tile), 0)
    q_blk = i * blocks_per_tile + lax.broadcasted_iota(I32, (nb, tile), 1) // MOBA_BLOCK
    for h in range(N_HEADS):
        pair = qnt[(h // 2) * 2 * HEAD_DIM:(h // 2 + 1) * 2 * HEAD_DIM]
        qt_ref[0, h] = jnp.where(half == h % 2, pair, 0.0).astype(BF16)
        for blk in range(blocks_per_tile):
            vt_ref[0, h, blk] = vt[h * HEAD_DIM:(h + 1) * HEAD_DIM,
                                   blk * MOBA_BLOCK:(blk + 1) * MOBA_BLOCK].astype(BF16)
        hs = slice(h * HEAD_DIM, (h + 1) * HEAD_DIM)
        gate = _mmh_nt(kmean[:, hs], qn[:, hs])
        work = jnp.where(n_iota < q_blk, gate, -jnp.inf)
        bits = jnp.zeros((1, tile), I32)
        for _ in range(MOBA_TOPK):
            m = jnp.max(work, axis=0, keepdims=True)
            idx = jnp.min(jnp.where(work == m, n_iota, nb), axis=0, keepdims=True)
            hit = n_iota == idx
            take = jnp.logical_and(hit, m > -jnp.inf)
            bits = bits | jnp.sum(jnp.where(take, jnp.left_shift(1, n_iota), 0), axis=0, keepdims=True)
            work = jnp.where(hit, -jnp.inf, work)
        sel_ref[0, h] = bits


def _moba_prep_call(q, k, v, q_norm_g, k_norm_g):
    bsz, s, w = q.shape
    tile = MOBA_PREP_TILE
    nb = s // MOBA_BLOCK
    assert nb <= 32, "block selection is a 32-bit mask"
    tok = lambda b, i: (b, i, 0)
    const = lambda b, i: (0, 0)
    return pl.pallas_call(
        _moba_prep_kernel,
        grid=(bsz, s // tile),
        in_specs=[pl.BlockSpec((1, tile, w), tok)] * 3 + [pl.BlockSpec((1, w), const)] * 2,
        out_specs=[pl.BlockSpec((1, N_HEADS, 2 * HEAD_DIM, tile), lambda b, i: (b, 0, 0, i)),
                   pl.BlockSpec((1, tile, w), tok),
                   pl.BlockSpec((1, N_HEADS, tile // MOBA_BLOCK, HEAD_DIM, MOBA_BLOCK),
                                lambda b, i: (b, 0, i, 0, 0)),
                   pl.BlockSpec((1, N_HEADS, 1, tile), lambda b, i: (b, 0, 0, i))],
        out_shape=[jax.ShapeDtypeStruct((bsz, N_HEADS, 2 * HEAD_DIM, s), BF16),
                   jax.ShapeDtypeStruct((bsz, s, w), BF16),
                   jax.ShapeDtypeStruct((bsz, N_HEADS, nb, HEAD_DIM, MOBA_BLOCK), BF16),
                   jax.ShapeDtypeStruct((bsz, N_HEADS, 1, s), I32)],
        scratch_shapes=[pltpu.VMEM((nb, w), F32)],
        compiler_params=pltpu.CompilerParams(dimension_semantics=("arbitrary", "arbitrary"),
                                             vmem_limit_bytes=VMEM_LIMIT_BYTES),
        name="moba_prep",
    )(q, k, v, jnp.tile(q_norm_g, N_HEADS).reshape(1, w), jnp.tile(k_norm_g, N_HEADS).reshape(1, w))


MOBA_KV_UNROLL = 4


def _moba_attn_kernel(qt_ref, k_ref, vt_ref, sel_ref, o_ref):
    qb = pl.program_id(2)
    bl = MOBA_BLOCK
    nb = vt_ref.shape[2]
    kpos = lax.broadcasted_iota(I32, (bl, bl), 0)
    qpos = lax.broadcasted_iota(I32, (bl, bl), 1)

    def key_block(j):
        return k_ref[0, pl.ds(pl.multiple_of(j * bl, bl), bl), :]

    k_own = key_block(qb)
    state = []
    for e in range(2):
        s = jnp.where(kpos <= qpos, jnp.dot(k_own, qt_ref[0, e], preferred_element_type=F32), NEG_INF)
        m = jnp.max(s, axis=0, keepdims=True)
        p = jnp.exp(s - m)
        l = jnp.sum(p, axis=0, keepdims=True)
        acc = jnp.dot(vt_ref[0, e, qb], p.astype(BF16), preferred_element_type=F32)
        state += [m, l, acc]

    def body(it, carry):
        carry = list(carry)
        js = [it * MOBA_KV_UNROLL + u for u in range(MOBA_KV_UNROLL)]
        jc = [jnp.minimum(j, nb - 1) for j in js]
        ks = [key_block(j) for j in jc]
        raw = [[jnp.dot(kj, qt_ref[0, e], preferred_element_type=F32) for kj in ks] for e in range(2)]
        probs = []
        for e in range(2):
            m, l, acc = carry[3 * e:3 * e + 3]
            sel = sel_ref[0, e]
            ss = []
            for j, j_in, s in zip(js, jc, raw[e]):
                picked = jnp.logical_and((jnp.right_shift(sel, j_in) & 1) != 0, j < qb)
                ss.append(jnp.where(picked, s, NEG_INF))
            m_new = m
            for s in ss:
                m_new = jnp.maximum(m_new, jnp.max(s, axis=0, keepdims=True))
            alpha = jnp.exp(m - m_new)
            ps = [jnp.exp(s - m_new) for s in ss]
            l = alpha * l
            for p in ps:
                l = l + jnp.sum(p, axis=0, keepdims=True)
            probs.append([p.astype(BF16) for p in ps])
            carry[3 * e:3 * e + 3] = [m_new, l, alpha * acc]
        for e in range(2):
            pv = [jnp.dot(vt_ref[0, e, j], p, preferred_element_type=F32) for j, p in zip(jc, probs[e])]
            carry[3 * e + 2] = carry[3 * e + 2] + sum(pv[1:], pv[0])
        return tuple(carry)

    trips = (qb + MOBA_KV_UNROLL - 1) // MOBA_KV_UNROLL
    state = lax.fori_loop(0, trips, body, tuple(state))
    for e in range(2):
        o_ref[0, e] = state[3 * e + 2] / state[3 * e + 1]


def _moba_attn_call(qt, kn, vt, sel):
    bsz, nh, _, s = qt.shape
    nb = s // MOBA_BLOCK
    return pl.pallas_call(
        _moba_attn_kernel,
        grid=(bsz, nh // 2, nb),
        in_specs=[pl.BlockSpec((1, 2, 2 * HEAD_DIM, MOBA_BLOCK), lambda b, h, i: (b, h, 0, i)),
                  pl.BlockSpec((1, s, 2 * HEAD_DIM), lambda b, h, i: (b, 0, h)),
                  pl.BlockSpec((1, 2, nb, HEAD_DIM, MOBA_BLOCK), lambda b, h, i: (b, h, 0, 0, 0)),
                  pl.BlockSpec((1, 2, 1, MOBA_BLOCK), lambda b, h, i: (b, h, 0, i))],
        out_specs=pl.BlockSpec((1, 2, HEAD_DIM, MOBA_BLOCK), lambda b, h, i: (b, h, 0, i)),
        out_shape=jax.ShapeDtypeStruct((bsz, nh, HEAD_DIM, s), F32),
        compiler_params=pltpu.CompilerParams(
            dimension_semantics=("arbitrary", "arbitrary", "arbitrary"),
            vmem_limit_bytes=VMEM_LIMIT_BYTES),
        name="moba_attn",
    )(qt, kn, vt, sel)


def _moba_branch(q, k, v, q_norm_g, k_norm_g):
    qt, kn, vt, sel = _moba_prep_call(q, k, v, q_norm_g, k_norm_g)
    yt = _moba_attn_call(qt, kn, vt, sel)
    return yt.reshape(yt.shape[0], WIDTH, yt.shape[3])


def _merge_kernel(x_ref, ya_ref, ybt_ref, gates_ref, wb0_ref, wb1_ref, wo_ref, g1_ref, n2_ref, sc2_ref,
                  sh2_ref, wq_ref, x1_ref, h2_ref, q_ref):
    d = x_ref.shape[2]
    ys_a = _mm(ya_ref[0], wb0_ref[...])
    ys_b = _mm_tn(ybt_ref[0], wb1_ref[...])
    gates = gates_ref[0]
    mixed = gates[:, 0:d] * ys_a + gates[:, d:2 * d] * ys_b
    x1 = x_ref[0] + g1_ref[0] * _mm(mixed, wo_ref[...])
    x1_ref[0] = x1
    h2 = _modulated_norm(x1, n2_ref[...], sc2_ref[0], sh2_ref[0])
    h2_ref[0] = h2
    q_ref[0] = _mm(h2, wq_ref[...])


def _merge_call(x, y_a, y_bt, gates, w_b0, w_b1, w_out, gate1, norm2_g, scale2, shift2, w_peer_q):
    bsz, s, d = x.shape
    tm = 256
    nq = w_peer_q.shape[1]
    tok = lambda b, i: (b, i, 0)
    per_b = lambda b, i: (b, 0, 0)
    const = lambda b, i: (0, 0)
    return pl.pallas_call(
        _merge_kernel,
        grid=(bsz, s // tm),
        in_specs=[pl.BlockSpec((1, tm, d), tok),
                  pl.BlockSpec((1, tm, WIDTH), tok),
                  pl.BlockSpec((1, WIDTH, tm), lambda b, i: (b, 0, i)),
                  pl.BlockSpec((1, tm, 2 * d), tok),
                  pl.BlockSpec((WIDTH, d), const),
                  pl.BlockSpec((WIDTH, d), const),
                  pl.BlockSpec((d, d), const),
                  pl.BlockSpec((1, 1, d), per_b),
                  pl.BlockSpec((1, d), const),
                  pl.BlockSpec((1, 1, d), per_b),
                  pl.BlockSpec((1, 1, d), per_b),
                  pl.BlockSpec((d, nq), const)],
        out_specs=[pl.BlockSpec((1, tm, d), tok), pl.BlockSpec((1, tm, d), tok),
                   pl.BlockSpec((1, tm, nq), tok)],
        out_shape=[jax.ShapeDtypeStruct((bsz, s, d), F32), jax.ShapeDtypeStruct((bsz, s, d), F32),
                   jax.ShapeDtypeStruct((bsz, s, nq), F32)],
        compiler_params=pltpu.CompilerParams(dimension_semantics=("arbitrary", "arbitrary"),
                                             vmem_limit_bytes=VMEM_LIMIT_BYTES),
        name="merge",
    )(x, y_a, y_bt, gates, w_b0, w_b1, w_out, gate1, norm2_g.reshape(1, d), scale2, shift2, w_peer_q)


def _topk_rows(x, k, payload=None):
    n = x.shape[0]
    iota = lax.broadcasted_iota(I32, x.shape, 0)
    vals, picks = [], []
    work = x
    for _ in range(k):
        m = jnp.max(work, axis=0, keepdims=True)
        idx = jnp.min(jnp.where(work == m, iota, n), axis=0, keepdims=True)
        hit = iota == idx
        vals.append(m)
        picks.append(idx if payload is None else jnp.max(jnp.where(hit, payload, -1), axis=0, keepdims=True))
        work = jnp.where(hit, -jnp.inf, work)
    return jnp.concatenate(vals, axis=0), jnp.concatenate(picks, axis=0)


def _route_kernel(q_ref, keys_ref, code_ref, gw_ref):
    k = PEER_TOPK
    eids, gws = [], []
    for hp in range(PEER_HEADS):
        q1 = q_ref[:, (2 * hp) * PEER_KEYS:(2 * hp + 1) * PEER_KEYS]
        q2 = q_ref[:, (2 * hp + 1) * PEER_KEYS:(2 * hp + 2) * PEER_KEYS]
        s1 = _mm_nt(keys_ref[0, hp], q1)
        s2 = _mm_nt(keys_ref[1, hp], q2)
        v1, i1 = _topk_rows(s1, k)
        v2, i2 = _topk_rows(s2, k)
        widths = [k // (a + 1) for a in range(k)]
        pad = -sum(widths) % SUBLANES
        comb = jnp.concatenate([v1[a:a + 1] + v2[0:w] for a, w in enumerate(widths)]
                               + [jnp.full((pad, v1.shape[1]), -jnp.inf, F32)], axis=0)
        expert = jnp.concatenate([i1[a:a + 1] * PEER_KEYS + i2[0:w] for a, w in enumerate(widths)]
                                 + [jnp.zeros((pad, v1.shape[1]), I32)], axis=0)
        sc, eid = _topk_rows(comb, k, payload=expert)
        e = jnp.exp(sc - sc[0:1])
        gws.append(e / jnp.sum(e, axis=0, keepdims=True))
        eids.append(eid)
    code_ref[...] = jnp.concatenate(eids, axis=0).T
    gw_ref[...] = jnp.concatenate(gws, axis=0).T


def _route_call(q, sub_keys):
    t, nq = q.shape
    tt = 256
    return pl.pallas_call(
        _route_kernel,
        grid=(t // tt,),
        in_specs=[pl.BlockSpec((tt, nq), lambda i: (i, 0)),
                  pl.BlockSpec(sub_keys.shape, lambda i: (0, 0, 0, 0))],
        out_specs=[pl.BlockSpec((tt, PEER_PICKS), lambda i: (i, 0)),
                   pl.BlockSpec((tt, PEER_PICKS), lambda i: (i, 0))],
        out_shape=[jax.ShapeDtypeStruct((t, PEER_PICKS), I32), jax.ShapeDtypeStruct((t, PEER_PICKS), F32)],
        compiler_params=pltpu.CompilerParams(dimension_semantics=("arbitrary",),
                                             vmem_limit_bytes=VMEM_LIMIT_BYTES),
        name="route",
    )(q, sub_keys)


PEER_TOKENS_PER_STEP = 64
PEER_ACCUMULATORS = 4


def _tile_table(tab):
    n, d = tab.shape
    return tab.astype(BF16).reshape(n, d // LANES, LANES)


def _gelu(x):
    return 0.5 * x * (1.0 + lax.erf(x * (2.0 ** -0.5)))


def _sublane_sums(tiles):
    row = lax.broadcasted_iota(I32, (SUBLANES, LANES), 0)
    odd, upper_pair, upper_half = (row & 1) != 0, (row & 2) != 0, row >= 4

    def quad(a, b, c, d):
        ab = jnp.where(odd, a + pltpu.roll(a, 1, 0), b + pltpu.roll(b, SUBLANES - 1, 0))
        cd = jnp.where(odd, c + pltpu.roll(c, 1, 0), d + pltpu.roll(d, SUBLANES - 1, 0))
        q = jnp.where(upper_pair, ab + pltpu.roll(ab, 2, 0), cd + pltpu.roll(cd, SUBLANES - 2, 0))
        return q + pltpu.roll(q, 4, 0)

    return jnp.where(upper_half, quad(tiles[7], tiles[6], tiles[5], tiles[4]),
                     quad(tiles[3], tiles[2], tiles[1], tiles[0]))


def _peer_act_kernel(code_ref, h_ref, gw_ref, tab_ref, act_ref, psum_ref):
    tokens = h_ref.shape[0]

    def token(t, carry):
        h = h_ref[t]
        base = pl.multiple_of(t * PEER_PICKS, PEER_PICKS)
        for j0 in range(0, PEER_PICKS, SUBLANES):
            prods = [tab_ref[code_ref[base + j0 + j]].astype(F32) * h for j in range(SUBLANES)]
            psum_ref[pl.ds(base + j0, SUBLANES), :] = _sublane_sums(prods)
        return carry

    lax.fori_loop(0, tokens, token, 0)
    ones = jnp.ones((SUBLANES, LANES), BF16)
    dots = jnp.concatenate(
        [_mm_nt(ones, psum_ref[t * PEER_PICKS:(t + 1) * PEER_PICKS, :])[0:1] for t in range(tokens)], axis=0)
    act_ref[...] = _gelu(dots) * gw_ref[...]


def _peer_act_call(code, h2, gw, tab):
    t = h2.shape[0]
    tt = PEER_TOKENS_PER_STEP
    return pl.pallas_call(
        _peer_act_kernel,
        grid=(t // tt,),
        in_specs=[pl.BlockSpec((tt * PEER_PICKS,), lambda i: (i,), memory_space=pltpu.SMEM),
                  pl.BlockSpec((tt, SUBLANES, LANES), lambda i: (i, 0, 0)),
                  pl.BlockSpec((tt, PEER_PICKS), lambda i: (i, 0)),
                  pl.BlockSpec(tab.shape, lambda i: (0, 0, 0), pipeline_mode=pl.Buffered(1))],
        out_specs=pl.BlockSpec((tt, PEER_PICKS), lambda i: (i, 0)),
        out_shape=jax.ShapeDtypeStruct((t, PEER_PICKS), F32),
        scratch_shapes=[pltpu.VMEM((tt * PEER_PICKS, LANES), F32)],
        compiler_params=pltpu.CompilerParams(dimension_semantics=("arbitrary",),
                                             vmem_limit_bytes=VMEM_LIMIT_BYTES),
        name="peer_act",
    )(code, h2, gw, tab)


def _peer_out_kernel(code_ref, act_ref, x_ref, g2_ref, tab_ref, o_ref, wrep_ref):
    tokens = x_ref.shape[0]
    g2 = g2_ref[0]
    eye = (lax.broadcasted_iota(I32, (PEER_PICKS, PEER_PICKS), 0)
           == lax.broadcasted_iota(I32, (PEER_PICKS, PEER_PICKS), 1)).astype(F32)
    ones = jnp.ones((PEER_PICKS, LANES), BF16)
    for t in range(tokens):
        wrep_ref[t * PEER_PICKS:(t + 1) * PEER_PICKS, :] = _mm_split(eye * act_ref[t:t + 1, :], ones)

    def token(t, carry):
        half = PEER_PICKS // 2

        def group(g, accs):
            accs = list(accs)
            base = pl.multiple_of(t * PEER_PICKS + g * half, half)
            for j in range(half):
                w = jnp.broadcast_to(wrep_ref[pl.ds(base + j, 1), :], (SUBLANES, LANES))
                accs[j % PEER_ACCUMULATORS] += w * tab_ref[code_ref[base + j]].astype(F32)
            return tuple(accs)

        zero = jnp.zeros((SUBLANES, LANES), F32)
        accs = lax.fori_loop(0, 2, group, (zero,) * PEER_ACCUMULATORS)
        o_ref[t] = x_ref[t] + g2 * ((accs[0] + accs[1]) + (accs[2] + accs[3]))
        return carry

    lax.fori_loop(0, tokens, token, 0)


def _peer_out_call(code, act, x1, gate2, tab, tokens_per_batch):
    t = x1.shape[0]
    tt = PEER_TOKENS_PER_STEP
    steps_per_batch = tokens_per_batch // tt
    return pl.pallas_call(
        _peer_out_kernel,
        grid=(t // tt,),
        in_specs=[pl.BlockSpec((tt * PEER_PICKS,), lambda i: (i,), memory_space=pltpu.SMEM),
                  pl.BlockSpec((tt, PEER_PICKS), lambda i: (i, 0)),
                  pl.BlockSpec((tt, SUBLANES, LANES), lambda i: (i, 0, 0)),
                  pl.BlockSpec((1, SUBLANES, LANES), lambda i: (i // steps_per_batch, 0, 0)),
                  pl.BlockSpec(tab.shape, lambda i: (0, 0, 0), pipeline_mode=pl.Buffered(1))],
        out_specs=pl.BlockSpec((tt, SUBLANES, LANES), lambda i: (i, 0, 0)),
        out_shape=jax.ShapeDtypeStruct((t, SUBLANES, LANES), F32),
        scratch_shapes=[pltpu.VMEM((tt * PEER_PICKS, LANES), F32)],
        compiler_params=pltpu.CompilerParams(dimension_semantics=("arbitrary",),
                                             vmem_limit_bytes=VMEM_LIMIT_BYTES),
        name="peer_out",
    )(code, act, x1, gate2, tab)


def _peer_ffn_residual(x1, h2, q, gate2, sub_keys, peer_u, peer_v):
    bsz, s, d = x1.shape
    t = bsz * s
    code, gw = _route_call(q.reshape(t, -1), sub_keys.astype(BF16))
    code = code.reshape(-1)
    act = _peer_act_call(code, h2.reshape(t, SUBLANES, LANES), gw, _tile_table(peer_u))
    out = _peer_out_call(code, act, x1.reshape(t, SUBLANES, LANES),
                         gate2.reshape(bsz, SUBLANES, LANES), _tile_table(peer_v), s)
    return out.reshape(bsz, s, d)


def kernel(x, c, w_ada, b_ada, norm1_g, w_in, mu_rwkv, w0, w2_decay, a0, a2_iclr, g2_gate, k_k, k_a,
           r_k, lnx_w, lnx_b, q_norm_g, k_norm_g, w_branch, w_out, norm2_g, w_peer_q, peer_sub_keys,
           peer_u, peer_v):
    bsz, s, d = x.shape
    ada = _ada_call(c, w_ada, b_ada).reshape(bsz, 6, 1, d)
    shift1, scale1, gate1, shift2, scale2, gate2 = (ada[:, j] for j in range(6))
    w_r = _rwkv_weight_cols(w_in[:, 0:1824]).astype(BF16)
    w_q, w_k, w_v = (w_in[:, 1824 + j * WIDTH:1824 + (j + 1) * WIDTH].astype(BF16) for j in range(3))
    w_g = w_in[:, 1824 + 3 * WIDTH:].astype(BF16)
    p_r, q, k, v, gates = _proj_call(x, norm1_g, scale1, shift1, w_r, w_q, w_k, w_v, w_g)
    y_a = _rwkv_call(p_r, *_rwkv_params(mu_rwkv, w0, w2_decay, a0, a2_iclr, g2_gate, k_k, k_a, r_k,
                                        lnx_w, lnx_b))
    y_bt = _moba_branch(q, k, v, q_norm_g, k_norm_g)
    x1, h2, pq = _merge_call(x, y_a, y_bt, gates, w_branch[0].astype(BF16), w_branch[1].astype(BF16),
                             w_out.astype(BF16), gate1, norm2_g, scale2, shift2, w_peer_q.astype(BF16))
    return _peer_ffn_residual(x1, h2, pq, gate2, peer_sub_keys, peer_u, peer_v)
```

```python
import functools

import jax
import jax.numpy as jnp
from jax import lax
from jax.experimental import pallas as pl
from jax.experimental.pallas import tpu as pltpu

F32 = jnp.float32
BF16 = jnp.bfloat16
I32 = jnp.int32

LANES = 128
SUBLANES = 8
VMEM_LIMIT_BYTES = 56 * 1024 * 1024

NORM_EPS = 1e-6
LNX_EPS = 64e-5
NEG_INF = -1e30

HEAD_DIM = 64
N_HEADS = 8
WIDTH = N_HEADS * HEAD_DIM
RWKV_CHUNK = 64
RWKV_TILE = 128
RWKV_PCOLS = 2048
MOBA_BLOCK = 256
MOBA_TOPK = 3
PEER_HEADS = 8
PEER_KEYS = 128
PEER_TOPK = 16
PEER_PICKS = PEER_HEADS * PEER_TOPK


def _mm(a, b):
    return jnp.dot(a.astype(BF16), b.astype(BF16), preferred_element_type=F32)


def _mm_nt(a, b):
    return lax.dot_general(a.astype(BF16), b.astype(BF16), (((1,), (1,)), ((), ())),
                           preferred_element_type=F32)


def _mm_tn(a, b):
    return lax.dot_general(a.astype(BF16), b.astype(BF16), (((0,), (0,)), ((), ())),
                           preferred_element_type=F32)


def _mm_split(a, b):
    hi = a.astype(BF16)
    lo = (a - hi.astype(F32)).astype(BF16)
    b = b.astype(BF16)
    return (jnp.dot(hi, b, preferred_element_type=F32) + jnp.dot(lo, b, preferred_element_type=F32))


def _mmh(a, b):
    return jnp.dot(a, b, preferred_element_type=F32, precision=lax.Precision.HIGHEST)


def _head_ones(n):
    r = lax.broadcasted_iota(I32, (n, n), 0) // HEAD_DIM
    c = lax.broadcasted_iota(I32, (n, n), 1) // HEAD_DIM
    return (r == c).astype(F32)


def _sigmoid(x):
    return 1.0 / (1.0 + jnp.exp(-x))


def _ada_kernel(c_ref, w_ref, b_ref, o_ref):
    c = c_ref[...]
    o_ref[...] = _mmh(c * _sigmoid(c), w_ref[...]) + b_ref[...]


def _ada_call(c, w_ada, b_ada):
    bsz, d = c.shape
    n = w_ada.shape[1]
    tn = 1536
    return pl.pallas_call(
        _ada_kernel,
        grid=(n // tn,),
        in_specs=[pl.BlockSpec((bsz, d), lambda j: (0, 0)),
                  pl.BlockSpec((d, tn), lambda j: (0, j)),
                  pl.BlockSpec((1, tn), lambda j: (0, j))],
        out_specs=pl.BlockSpec((bsz, tn), lambda j: (0, j)),
        out_shape=jax.ShapeDtypeStruct((bsz, n), F32),
        compiler_params=pltpu.CompilerParams(dimension_semantics=("arbitrary",),
                                             vmem_limit_bytes=VMEM_LIMIT_BYTES),
        name="ada",
    )(c, w_ada, b_ada.reshape(1, n))


def _rwkv_kernel(p_ref, pprev_ref, mu_ref, vec_ref, w2_ref, a2_ref, g2_ref, y_ref, st_ref):
    i = pl.program_id(1)
    tile = p_ref.shape[1]
    L = RWKV_CHUNK

    @pl.when(i == 0)
    def _():
        st_ref[...] = jnp.zeros_like(st_ref)

    p = p_ref[0]
    prev_last = jnp.where(i == 0, 0.0, pprev_ref[0][SUBLANES - 1:SUBLANES, :])
    row = lax.broadcasted_iota(I32, p.shape, 0)
    p_prev = jnp.where(row == 0, prev_last, pltpu.roll(p, 1, 0))
    xm = p + (p_prev - p) * mu_ref[...]
    r = xm[:, 0:WIDTH]
    k = xm[:, WIDTH:2 * WIDTH]
    v = xm[:, 2 * WIDTH:3 * WIDTH]
    wl = xm[:, 3 * WIDTH:3 * WIDTH + 128]
    al = xm[:, 3 * WIDTH + 128:3 * WIDTH + 256]
    gl = xm[:, 3 * WIDTH + 256:3 * WIDTH + 512]
    w0, a0, k_k, k_a = vec_ref[0:1, :], vec_ref[1:2, :], vec_ref[2:3, :], vec_ref[3:4, :]
    r_k, lnx_w, lnx_b = vec_ref[4:5, :], vec_ref[5:6, :], vec_ref[6:7, :]

    z = -(w0 + _mmh(jnp.tanh(wl), w2_ref[...]))
    softplus = jnp.maximum(z, 0.0) + jnp.log(1.0 + jnp.exp(-jnp.abs(z)))
    lw = -jnp.exp(-softplus - 0.5)
    a = _sigmoid(a0 + _mmh(al, a2_ref[...]))
    g = _mmh(_sigmoid(gl), g2_ref[...])
    hones = _head_ones(WIDTH)
    kk = k * k_k
    kk = kk / jnp.maximum(jnp.sqrt(_mm(kk * kk, hones)), 1e-12)
    k2 = k * (1.0 + (a - 1.0) * k_a)
    a_s = -kk
    b_s = kk * a

    li = lax.broadcasted_iota(I32, (L, L), 0)
    lj = lax.broadcasted_iota(I32, (L, L), 1)
    tril_incl = li >= lj
    tril_strict = li > lj
    eye = (li == lj).astype(F32)
    csum = tril_incl.astype(F32)

    n_chunks = tile // L
    items = [(c, h) for c in range(n_chunks) for h in range(N_HEADS)]
    zeros_ll = jnp.zeros((L, L), F32)
    per_chunk = []
    for c in range(n_chunks):
        sl = slice(c * L, (c + 1) * L)
        lw_c = lw[sl]
        cum = _mmh(csum, lw_c)
        ref = cum[L // 2 - 1:L // 2, :]
        g_inv = jnp.exp(ref - cum)
        g_end = jnp.exp(cum[L - 1:L, :] - ref)
        bt = b_s[sl] * g_inv
        kt = k2[sl] * g_inv
        per_chunk.append(dict(
            at=a_s[sl] * jnp.exp(cum - lw_c - ref), rt=r[sl] * jnp.exp(cum - ref), bt=bt, kt=kt,
            bbar=bt * g_end, kbar=kt * g_end, v=v[sl], g_tot=jnp.exp(cum[L - 1:L, :]), g_ref=jnp.exp(ref)))

    def head(c, h, name):
        return per_chunk[c][name][:, h * HEAD_DIM:(h + 1) * HEAD_DIM]

    sc = [_mm_nt(jnp.concatenate([head(c, h, "at"), head(c, h, "rt")], axis=0),
                 jnp.concatenate([head(c, h, "bt"), head(c, h, "kt")], axis=0)) for c, h in items]
    m_ab = [jnp.where(tril_strict, s[0:L, 0:L], 0.0) for s in sc]
    m_ak = [jnp.where(tril_strict, s[0:L, L:2 * L], 0.0) for s in sc]
    tril_incl2 = (lax.broadcasted_iota(I32, (L, 2 * L), 0)
                  >= lax.broadcasted_iota(I32, (L, 2 * L), 1) % L)
    g_rbk = [jnp.where(tril_incl2, s[L:2 * L, :], 0.0) for s in sc]
    akv = [_mm(m, head(c, h, "v")) for m, (c, h) in zip(m_ak, items)]
    pw = [_mm(m, m) for m in m_ab]
    t_inv = [eye + m for m in m_ab]
    for _ in range(4):
        both = [_mm(jnp.concatenate([p, t], axis=0), p) for p, t in zip(pw, t_inv)]
        t_inv = [t + b[L:2 * L] for t, b in zip(t_inv, both)]
        pw = [b[0:L] for b in both]
    t_inv = [t + _mm(t, p) for t, p in zip(t_inv, pw)]
    tw = [_mm(t, jnp.concatenate([head(c, h, "at"), x], axis=1)) for t, x, (c, h) in zip(t_inv, akv, items)]
    ro = [_mm(g, jnp.concatenate([w, jnp.concatenate([zeros_ll, head(c, h, "v")], axis=1)], axis=0))
          for g, w, (c, h) in zip(g_rbk, tw, items)]
    a_hat = [w[:, 0:L] * head(c, h, "g_ref") for w, (c, h) in zip(tw, items)]
    r_hat = [(head(c, h, "rt") + x[:, 0:L]) * head(c, h, "g_ref") for x, (c, h) in zip(ro, items)]
    pp = [_mm_tn(jnp.concatenate([head(c, h, "bbar"), head(c, h, "kbar")], axis=0),
                 jnp.concatenate([jnp.concatenate([a, w[:, L:2 * L]], axis=1),
                                  jnp.concatenate([zeros_ll, head(c, h, "v")], axis=1)], axis=0))
          for a, w, (c, h) in zip(a_hat, tw, items)]
    state = [st_ref[h] for h in range(N_HEADS)]
    outs = []
    for c in range(n_chunks):
        head_outs = []
        for h in range(N_HEADS):
            i = c * N_HEADS + h
            phi_t = eye * head(c, h, "g_tot") + pp[i][:, 0:L]
            both = _mm(jnp.concatenate([r_hat[i], phi_t], axis=0), state[h])
            head_outs.append(both[0:L] + ro[i][:, L:2 * L])
            state[h] = both[L:2 * L] + pp[i][:, L:2 * L]
        outs.append(jnp.concatenate(head_outs, axis=1))
    for h in range(N_HEADS):
        st_ref[h] = state[h]
    out = jnp.concatenate(outs, axis=0)

    inv_n = 1.0 / HEAD_DIM
    mean = _mm(out, hones) * inv_n
    cen = out - mean
    var = _mm(cen * cen, hones) * inv_n
    out = cen * lax.rsqrt(var + LNX_EPS) * lnx_w + lnx_b
    bonus = _mm(r * k2 * r_k, hones) * v
    y_ref[0] = (out + bonus) * g


def _mmh_nt(a, b):
    return lax.dot_general(a, b, (((1,), (1,)), ((), ())), preferred_element_type=F32,
                           precision=lax.Precision.HIGHEST)


def _mmh_tn(a, b):
    return lax.dot_general(a, b, (((0,), (0,)), ((), ())), preferred_element_type=F32,
                           precision=lax.Precision.HIGHEST)


def _rwkv_call(p, mu, vecs, w2, a2, g2):
    bsz, s, pc = p.shape
    tile = RWKV_TILE
    const = lambda b, i: (0, 0)
    return pl.pallas_call(
        _rwkv_kernel,
        grid=(bsz, s // tile),
        in_specs=[pl.BlockSpec((1, tile, pc), lambda b, i: (b, i, 0)),
                  pl.BlockSpec((1, SUBLANES, pc),
                               lambda b, i: (b, jnp.maximum(i * (tile // SUBLANES) - 1, 0), 0)),
                  pl.BlockSpec((1, pc), const),
                  pl.BlockSpec((SUBLANES, WIDTH), const),
                  pl.BlockSpec((128, WIDTH), const),
                  pl.BlockSpec((128, WIDTH), const),
                  pl.BlockSpec((256, WIDTH), const)],
        out_specs=pl.BlockSpec((1, tile, WIDTH), lambda b, i: (b, i, 0)),
        out_shape=jax.ShapeDtypeStruct((bsz, s, WIDTH), F32),
        scratch_shapes=[pltpu.VMEM((N_HEADS, HEAD_DIM, HEAD_DIM), F32)],
        compiler_params=pltpu.CompilerParams(dimension_semantics=("arbitrary", "arbitrary"),
                                             vmem_limit_bytes=VMEM_LIMIT_BYTES),
        name="rwkv",
    )(p, p, mu, vecs, w2, a2, g2)


def _pad_rows(w, n):
    return jnp.pad(w, ((0, n - w.shape[0]), (0, 0)))


def _rwkv_params(mu_rwkv, w0, w2_decay, a0, a2_iclr, g2_gate, k_k, k_a, r_k, lnx_w, lnx_b):
    mu = jnp.concatenate([mu_rwkv[0:1536], _pad1(mu_rwkv[1536:1600], 128),
                          _pad1(mu_rwkv[1600:1664], 128), _pad1(mu_rwkv[1664:1824], 256)])
    vecs = jnp.stack([w0, a0, k_k, k_a, r_k.reshape(-1), lnx_w, lnx_b, jnp.zeros_like(w0)])
    return (mu.reshape(1, RWKV_PCOLS), vecs, _pad_rows(w2_decay, 128), _pad_rows(a2_iclr, 128),
            _pad_rows(g2_gate, 256))


def _pad1(v, n):
    return jnp.pad(v, (0, n - v.shape[0]))


def _pad_last(x, n):
    return jnp.pad(x, [(0, 0)] * (x.ndim - 1) + [(0, n - x.shape[-1])])


def _pad_cols(w, n):
    return _pad_last(w, n)


def _rwkv_weight_cols(w):
    return jnp.concatenate([w[:, 0:1536], _pad_cols(w[:, 1536:1600], 128),
                            _pad_cols(w[:, 1600:1664], 128), _pad_cols(w[:, 1664:1824], 256)], axis=1)


def _modulated_norm(x, gain, scale, shift):
    y = x * lax.rsqrt(jnp.mean(x * x, axis=-1, keepdims=True) + NORM_EPS)
    return y * gain * (1.0 + scale) + shift


def _proj_kernel(x_ref, g_ref, sc_ref, sh_ref, wr_ref, wq_ref, wk_ref, wv_ref, wg_ref,
                 pr_ref, q_ref, k_ref, v_ref, gate_ref):
    h = _modulated_norm(x_ref[0], g_ref[...], sc_ref[0], sh_ref[0]).astype(BF16)
    pr_ref[0] = jnp.dot(h, wr_ref[...], preferred_element_type=F32)
    q_ref[0] = jnp.dot(h, wq_ref[...], preferred_element_type=F32)
    k_ref[0] = jnp.dot(h, wk_ref[...], preferred_element_type=F32)
    v_ref[0] = jnp.dot(h, wv_ref[...], preferred_element_type=F32)
    gate_ref[0] = _sigmoid(jnp.dot(h, wg_ref[...], preferred_element_type=F32))


def _proj_call(x, norm_g, scale, shift, w_r, w_q, w_k, w_v, w_g):
    bsz, s, d = x.shape
    tm = 512
    tok = lambda b, i: (b, i, 0)
    per_b = lambda b, i: (b, 0, 0)
    const = lambda b, i: (0, 0)
    outs = [(RWKV_PCOLS, F32), (WIDTH, F32), (WIDTH, F32), (WIDTH, F32), (2 * d, F32)]
    return pl.pallas_call(
        _proj_kernel,
        grid=(bsz, s // tm),
        in_specs=[pl.BlockSpec((1, tm, d), tok),
                  pl.BlockSpec((1, d), const),
                  pl.BlockSpec((1, 1, d), per_b),
                  pl.BlockSpec((1, 1, d), per_b)]
                 + [pl.BlockSpec(w.shape, const) for w in (w_r, w_q, w_k, w_v, w_g)],
        out_specs=[pl.BlockSpec((1, tm, n), tok) for n, _ in outs],
        out_shape=[jax.ShapeDtypeStruct((bsz, s, n), dt) for n, dt in outs],
        compiler_params=pltpu.CompilerParams(dimension_semantics=("arbitrary", "arbitrary"),
                                             vmem_limit_bytes=VMEM_LIMIT_BYTES),
        name="proj",
    )(x, norm_g.reshape(1, d), scale, shift, w_r, w_q, w_k, w_v, w_g)


MOBA_PREP_TILE = 2 * MOBA_BLOCK


def _moba_prep_kernel(q_ref, k_ref, v_ref, qg_ref, kg_ref, qt_ref, kn_ref, vt_ref, sel_ref, kmean_ref):
    i = pl.program_id(1)
    tile = q_ref.shape[1]
    nb = kmean_ref.shape[0]
    blocks_per_tile = tile // MOBA_BLOCK

    @pl.when(i == 0)
    def _():
        kmean_ref[...] = jnp.zeros_like(kmean_ref)

    q, k, v = q_ref[0], k_ref[0], v_ref[0]
    hones = _head_ones(WIDTH)
    inv_n = 1.0 / HEAD_DIM
    qn = q * lax.rsqrt(_mm_split(q * q, hones) * inv_n + NORM_EPS) * qg_ref[...] * (HEAD_DIM ** -0.5)
    kn = k * lax.rsqrt(_mm_split(k * k, hones) * inv_n + NORM_EPS) * kg_ref[...]
    kn_ref[0] = kn.astype(BF16)

    kmean = kmean_ref[...]
    brow = lax.broadcasted_iota(I32, kmean.shape, 0)
    for blk in range(blocks_per_tile):
        km = jnp.mean(kn[blk * MOBA_BLOCK:(blk + 1) * MOBA_BLOCK], axis=0, keepdims=True)
        kmean = jnp.where(brow == i * blocks_per_tile + blk, km, kmean)
    kmean_ref[...] = kmean

    qnt = qn.T
    vt = v.T
    half = lax.broadcasted_iota(I32, (2 * HEAD_DIM, tile), 0) // HEAD_DIM
    n_iota = lax.broadcasted_iota(I32, (nb, tile), 0)
    q_blk = i * blocks_per_tile + lax.broadcasted_iota(I32, (nb, tile), 1) // MOBA_BLOCK
    for h in range(N_HEADS):
        pair = qnt[(h // 2) * 2 * HEAD_DIM:(h // 2 + 1) * 2 * HEAD_DIM]
        qt_ref[0, h] = jnp.where(half == h % 2, pair, 0.0).astype(BF16)
        for blk in range(blocks_per_tile):
            vt_ref[0, h, blk] = vt[h * HEAD_DIM:(h + 1) * HEAD_DIM,
                                   blk * MOBA_BLOCK:(blk + 1) * MOBA_BLOCK].astype(BF16)
        hs = slice(h * HEAD_DIM, (h + 1) * HEAD_DIM)
        gate = _mmh_nt(kmean[:, hs], qn[:, hs])
        work = jnp.where(n_iota < q_blk, gate, -jnp.inf)
        bits = jnp.zeros((1, tile), I32)
        for _ in range(MOBA_TOPK):
            m = jnp.max(work, axis=0, keepdims=True)
            idx = jnp.min(jnp.where(work == m, n_iota, nb), axis=0, keepdims=True)
            hit = n_iota == idx
            take = jnp.logical_and(hit, m > -jnp.inf)
            bits = bits | jnp.sum(jnp.where(take, jnp.left_shift(1, n_iota), 0), axis=0, keepdims=True)
            work = jnp.where(hit, -jnp.inf, work)
        sel_ref[0, h] = bits


def _moba_prep_call(q, k, v, q_norm_g, k_norm_g):
    bsz, s, w = q.shape
    tile = MOBA_PREP_TILE
    nb = s // MOBA_BLOCK
    assert nb <= 32, "block selection is a 32-bit mask"
    tok = lambda b, i: (b, i, 0)
    const = lambda b, i: (0, 0)
    return pl.pallas_call(
        _moba_prep_kernel,
        grid=(bsz, s // tile),
        in_specs=[pl.BlockSpec((1, tile, w), tok)] * 3 + [pl.BlockSpec((1, w), const)] * 2,
        out_specs=[pl.BlockSpec((1, N_HEADS, 2 * HEAD_DIM, tile), lambda b, i: (b, 0, 0, i)),
                   pl.BlockSpec((1, tile, w), tok),
                   pl.BlockSpec((1, N_HEADS, tile // MOBA_BLOCK, HEAD_DIM, MOBA_BLOCK),
                                lambda b, i: (b, 0, i, 0, 0)),
                   pl.BlockSpec((1, N_HEADS, 1, tile), lambda b, i: (b, 0, 0, i))],
        out_shape=[jax.ShapeDtypeStruct((bsz, N_HEADS, 2 * HEAD_DIM, s), BF16),
                   jax.ShapeDtypeStruct((bsz, s, w), BF16),
                   jax.ShapeDtypeStruct((bsz, N_HEADS, nb, HEAD_DIM, MOBA_BLOCK), BF16),
                   jax.ShapeDtypeStruct((bsz, N_HEADS, 1, s), I32)],
        scratch_shapes=[pltpu.VMEM((nb, w), F32)],
        compiler_params=pltpu.CompilerParams(dimension_semantics=("arbitrary", "arbitrary"),
                                             vmem_limit_bytes=VMEM_LIMIT_BYTES),
        name="moba_prep",
    )(q, k, v, jnp.tile(q_norm_g, N_HEADS).reshape(1, w), jnp.tile(k_norm_g, N_HEADS).reshape(1, w))


MOBA_KV_UNROLL = 4


def _moba_attn_kernel(qt_ref, k_ref, vt_ref, sel_ref, o_ref):
    qb = pl.program_id(2)
    bl = MOBA_BLOCK
    nb = vt_ref.shape[2]
    kpos = lax.broadcasted_iota(I32, (bl, bl), 0)
    qpos = lax.broadcasted_iota(I32, (bl, bl), 1)

    def key_block(j):
        return k_ref[0, pl.ds(pl.multiple_of(j * bl, bl), bl), :]

    k_own = key_block(qb)
    state = []
    for e in range(2):
        s = jnp.where(kpos <= qpos, jnp.dot(k_own, qt_ref[0, e], preferred_element_type=F32), NEG_INF)
        m = jnp.max(s, axis=0, keepdims=True)
        p = jnp.exp(s - m)
        l = jnp.sum(p, axis=0, keepdims=True)
        acc = jnp.dot(vt_ref[0, e, qb], p.astype(BF16), preferred_element_type=F32)
        state += [m, l, acc]

    def body(it, carry):
        carry = list(carry)
        js = [it * MOBA_KV_UNROLL + u for u in range(MOBA_KV_UNROLL)]
        jc = [jnp.minimum(j, nb - 1) for j in js]
        ks = [key_block(j) for j in jc]
        raw = [[jnp.dot(kj, qt_ref[0, e], preferred_element_type=F32) for kj in ks] for e in range(2)]
        probs = []
        for e in range(2):
            m, l, acc = carry[3 * e:3 * e + 3]
            sel = sel_ref[0, e]
            ss = []
            for j, j_in, s in zip(js, jc, raw[e]):
                picked = jnp.logical_and((jnp.right_shift(sel, j_in) & 1) != 0, j < qb)
                ss.append(jnp.where(picked, s, NEG_INF))
            m_new = m
            for s in ss:
                m_new = jnp.maximum(m_new, jnp.max(s, axis=0, keepdims=True))
            alpha = jnp.exp(m - m_new)
            ps = [jnp.exp(s - m_new) for s in ss]
            l = alpha * l
            for p in ps:
                l = l + jnp.sum(p, axis=0, keepdims=True)
            probs.append([p.astype(BF16) for p in ps])
            carry[3 * e:3 * e + 3] = [m_new, l, alpha * acc]
        for e in range(2):
            pv = [jnp.dot(vt_ref[0, e, j], p, preferred_element_type=F32) for j, p in zip(jc, probs[e])]
            carry[3 * e + 2] = carry[3 * e + 2] + sum(pv[1:], pv[0])
        return tuple(carry)

    trips = (qb + MOBA_KV_UNROLL - 1) // MOBA_KV_UNROLL
    state = lax.fori_loop(0, trips, body, tuple(state))
    for e in range(2):
        o_ref[0, e] = state[3 * e + 2] / state[3 * e + 1]


def _moba_attn_call(qt, kn, vt, sel):
    bsz, nh, _, s = qt.shape
    nb = s // MOBA_BLOCK
    return pl.pallas_call(
        _moba_attn_kernel,
        grid=(bsz, nh // 2, nb),
        in_specs=[pl.BlockSpec((1, 2, 2 * HEAD_DIM, MOBA_BLOCK), lambda b, h, i: (b, h, 0, i)),
                  pl.BlockSpec((1, s, 2 * HEAD_DIM), lambda b, h, i: (b, 0, h)),
                  pl.BlockSpec((1, 2, nb, HEAD_DIM, MOBA_BLOCK), lambda b, h, i: (b, h, 0, 0, 0)),
                  pl.BlockSpec((1, 2, 1, MOBA_BLOCK), lambda b, h, i: (b, h, 0, i))],
        out_specs=pl.BlockSpec((1, 2, HEAD_DIM, MOBA_BLOCK), lambda b, h, i: (b, h, 0, i)),
        out_shape=jax.ShapeDtypeStruct((bsz, nh, HEAD_DIM, s), F32),
        compiler_params=pltpu.CompilerParams(
            dimension_semantics=("arbitrary", "arbitrary", "arbitrary"),
            vmem_limit_bytes=VMEM_LIMIT_BYTES),
        name="moba_attn",
    )(qt, kn, vt, sel)


def _moba_branch(q, k, v, q_norm_g, k_norm_g):
    qt, kn, vt, sel = _moba_prep_call(q, k, v, q_norm_g, k_norm_g)
    yt = _moba_attn_call(qt, kn, vt, sel)
    return yt.reshape(yt.shape[0], WIDTH, yt.shape[3])


def _merge_kernel(x_ref, ya_ref, ybt_ref, gates_ref, wb0_ref, wb1_ref, wo_ref, g1_ref, n2_ref, sc2_ref,
                  sh2_ref, wq_ref, x1_ref, h2_ref, q_ref):
    d = x_ref.shape[2]
    ys_a = _mm(ya_ref[0], wb0_ref[...])
    ys_b = _mm_tn(ybt_ref[0], wb1_ref[...])
    gates = gates_ref[0]
    mixed = gates[:, 0:d] * ys_a + gates[:, d:2 * d] * ys_b
    x1 = x_ref[0] + g1_ref[0] * _mm(mixed, wo_ref[...])
    x1_ref[0] = x1
    h2 = _modulated_norm(x1, n2_ref[...], sc2_ref[0], sh2_ref[0])
    h2_ref[0] = h2
    q_ref[0] = _mm(h2, wq_ref[...])


def _merge_call(x, y_a, y_bt, gates, w_b0, w_b1, w_out, gate1, norm2_g, scale2, shift2, w_peer_q):
    bsz, s, d = x.shape
    tm = 256
    nq = w_peer_q.shape[1]
    tok = lambda b, i: (b, i, 0)
    per_b = lambda b, i: (b, 0, 0)
    const = lambda b, i: (0, 0)
    return pl.pallas_call(
        _merge_kernel,
        grid=(bsz, s // tm),
        in_specs=[pl.BlockSpec((1, tm, d), tok),
                  pl.BlockSpec((1, tm, WIDTH), tok),
                  pl.BlockSpec((1, WIDTH, tm), lambda b, i: (b, 0, i)),
                  pl.BlockSpec((1, tm, 2 * d), tok),
                  pl.BlockSpec((WIDTH, d), const),
                  pl.BlockSpec((WIDTH, d), const),
                  pl.BlockSpec((d, d), const),
                  pl.BlockSpec((1, 1, d), per_b),
                  pl.BlockSpec((1, d), const),
                  pl.BlockSpec((1, 1, d), per_b),
                  pl.BlockSpec((1, 1, d), per_b),
                  pl.BlockSpec((d, nq), const)],
        out_specs=[pl.BlockSpec((1, tm, d), tok), pl.BlockSpec((1, tm, d), tok),
                   pl.BlockSpec((1, tm, nq), tok)],
        out_shape=[jax.ShapeDtypeStruct((bsz, s, d), F32), jax.ShapeDtypeStruct((bsz, s, d), F32),
                   jax.ShapeDtypeStruct((bsz, s, nq), F32)],
        compiler_params=pltpu.CompilerParams(dimension_semantics=("arbitrary", "arbitrary"),
                                             vmem_limit_bytes=VMEM_LIMIT_BYTES),
        name="merge",
    )(x, y_a, y_bt, gates, w_b0, w_b1, w_out, gate1, norm2_g.reshape(1, d), scale2, shift2, w_peer_q)


def _topk_rows(x, k, payload=None):
    n = x.shape[0]
    iota = lax.broadcasted_iota(I32, x.shape, 0)
    vals, picks = [], []
    work = x
    for _ in range(k):
        m = jnp.max(work, axis=0, keepdims=True)
        idx = jnp.min(jnp.where(work == m, iota, n), axis=0, keepdims=True)
        hit = iota == idx
        vals.append(m)
        picks.append(idx if payload is None else jnp.max(jnp.where(hit, payload, -1), axis=0, keepdims=True))
        work = jnp.where(hit, -jnp.inf, work)
    return jnp.concatenate(vals, axis=0), jnp.concatenate(picks, axis=0)


def _route_kernel(q_ref, keys_ref, code_ref, gw_ref):
    k = PEER_TOPK
    eids, gws = [], []
    for hp in range(PEER_HEADS):
        q1 = q_ref[:, (2 * hp) * PEER_KEYS:(2 * hp + 1) * PEER_KEYS]
        q2 = q_ref[:, (2 * hp + 1) * PEER_KEYS:(2 * hp + 2) * PEER_KEYS]
        s1 = _mm_nt(keys_ref[0, hp], q1)
        s2 = _mm_nt(keys_ref[1, hp], q2)
        v1, i1 = _topk_rows(s1, k)
        v2, i2 = _topk_rows(s2, k)
        widths = [k // (a + 1) for a in range(k)]
        pad = -sum(widths) % SUBLANES
        comb = jnp.concatenate([v1[a:a + 1] + v2[0:w] for a, w in enumerate(widths)]
                               + [jnp.full((pad, v1.shape[1]), -jnp.inf, F32)], axis=0)
        expert = jnp.concatenate([i1[a:a + 1] * PEER_KEYS + i2[0:w] for a, w in enumerate(widths)]
                                 + [jnp.zeros((pad, v1.shape[1]), I32)], axis=0)
        sc, eid = _topk_rows(comb, k, payload=expert)
        e = jnp.exp(sc - sc[0:1])
        gws.append(e / jnp.sum(e, axis=0, keepdims=True))
        eids.append(eid)
    code_ref[...] = jnp.concatenate(eids, axis=0).T
    gw_ref[...] = jnp.concatenate(gws, axis=0).T


def _route_call(q, sub_keys):
    t, nq = q.shape
    tt = 256
    return pl.pallas_call(
        _route_kernel,
        grid=(t // tt,),
        in_specs=[pl.BlockSpec((tt, nq), lambda i: (i, 0)),
                  pl.BlockSpec(sub_keys.shape, lambda i: (0, 0, 0, 0))],
        out_specs=[pl.BlockSpec((tt, PEER_PICKS), lambda i: (i, 0)),
                   pl.BlockSpec((tt, PEER_PICKS), lambda i: (i, 0))],
        out_shape=[jax.ShapeDtypeStruct((t, PEER_PICKS), I32), jax.ShapeDtypeStruct((t, PEER_PICKS), F32)],
        compiler_params=pltpu.CompilerParams(dimension_semantics=("arbitrary",),
                                             vmem_limit_bytes=VMEM_LIMIT_BYTES),
        name="route",
    )(q, sub_keys)


PEER_TOKENS_PER_STEP = 64
PEER_ACCUMULATORS = 4


def _tile_table(tab):
    n, d = tab.shape
    return tab.astype(BF16).reshape(n, d // LANES, LANES)


def _gelu(x):
    return 0.5 * x * (1.0 + lax.erf(x * (2.0 ** -0.5)))


def _sublane_sums(tiles):
    row = lax.broadcasted_iota(I32, (SUBLANES, LANES), 0)
    odd, upper_pair, upper_half = (row & 1) != 0, (row & 2) != 0, row >= 4

    def quad(a, b, c, d):
        ab = jnp.where(odd, a + pltpu.roll(a, 1, 0), b + pltpu.roll(b, SUBLANES - 1, 0))
        cd = jnp.where(odd, c + pltpu.roll(c, 1, 0), d + pltpu.roll(d, SUBLANES - 1, 0))
        q = jnp.where(upper_pair, ab + pltpu.roll(ab, 2, 0), cd + pltpu.roll(cd, SUBLANES - 2, 0))
        return q + pltpu.roll(q, 4, 0)

    return jnp.where(upper_half, quad(tiles[7], tiles[6], tiles[5], tiles[4]),
                     quad(tiles[3], tiles[2], tiles[1], tiles[0]))


def _peer_act_kernel(code_ref, h_ref, gw_ref, tab_ref, act_ref, psum_ref):
    tokens = h_ref.shape[0]

    def token(t, carry):
        h = h_ref[t]
        base = pl.multiple_of(t * PEER_PICKS, PEER_PICKS)
        for j0 in range(0, PEER_PICKS, SUBLANES):
            prods = [tab_ref[code_ref[base + j0 + j]].astype(F32) * h for j in range(SUBLANES)]
            psum_ref[pl.ds(base + j0, SUBLANES), :] = _sublane_sums(prods)
        return carry

    lax.fori_loop(0, tokens, token, 0)
    ones = jnp.ones((SUBLANES, LANES), BF16)
    dots = jnp.concatenate(
        [_mm_nt(ones, psum_ref[t * PEER_PICKS:(t + 1) * PEER_PICKS, :])[0:1] for t in range(tokens)], axis=0)
    act_ref[...] = _gelu(dots) * gw_ref[...]


def _peer_act_call(code, h2, gw, tab):
    t = h2.shape[0]
    tt = PEER_TOKENS_PER_STEP
    return pl.pallas_call(
        _peer_act_kernel,
        grid=(t // tt,),
        in_specs=[pl.BlockSpec((tt * PEER_PICKS,), lambda i: (i,), memory_space=pltpu.SMEM),
                  pl.BlockSpec((tt, SUBLANES, LANES), lambda i: (i, 0, 0)),
                  pl.BlockSpec((tt, PEER_PICKS), lambda i: (i, 0)),
                  pl.BlockSpec(tab.shape, lambda i: (0, 0, 0), pipeline_mode=pl.Buffered(1))],
        out_specs=pl.BlockSpec((tt, PEER_PICKS), lambda i: (i, 0)),
        out_shape=jax.ShapeDtypeStruct((t, PEER_PICKS), F32),
        scratch_shapes=[pltpu.VMEM((tt * PEER_PICKS, LANES), F32)],
        compiler_params=pltpu.CompilerParams(dimension_semantics=("arbitrary",),
                                             vmem_limit_bytes=VMEM_LIMIT_BYTES),
        name="peer_act",
    )(code, h2, gw, tab)


def _peer_out_kernel(code_ref, act_ref, x_ref, g2_ref, tab_ref, o_ref, wrep_ref):
    tokens = x_ref.shape[0]
    g2 = g2_ref[0]
    eye = (lax.broadcasted_iota(I32, (PEER_PICKS, PEER_PICKS), 0)
           == lax.broadcasted_iota(I32, (PEER_PICKS, PEER_PICKS), 1)).astype(F32)
    ones = jnp.ones((PEER_PICKS, LANES), BF16)
    for t in range(tokens):
        wrep_ref[t * PEER_PICKS:(t + 1) * PEER_PICKS, :] = _mm_split(eye * act_ref[t:t + 1, :], ones)

    def token(t, carry):
        groups = 4
        per = PEER_PICKS // groups

        def group(g, accs):
            accs = list(accs)
            base = pl.multiple_of(t * PEER_PICKS + g * per, per)
            for j in range(per):
                w = jnp.broadcast_to(wrep_ref[pl.ds(base + j, 1), :], (SUBLANES, LANES))
                accs[j % PEER_ACCUMULATORS] += w * tab_ref[code_ref[base + j]].astype(F32)
            return tuple(accs)

        zero = jnp.zeros((SUBLANES, LANES), F32)
        accs = lax.fori_loop(0, groups, group, (zero,) * PEER_ACCUMULATORS)
        o_ref[t] = x_ref[t] + g2 * ((accs[0] + accs[1]) + (accs[2] + accs[3]))
        return carry

    lax.fori_loop(0, tokens, token, 0)


def _peer_out_call(code, act, x1, gate2, tab, tokens_per_batch):
    t = x1.shape[0]
    tt = PEER_TOKENS_PER_STEP
    steps_per_batch = tokens_per_batch // tt
    return pl.pallas_call(
        _peer_out_kernel,
        grid=(t // tt,),
        in_specs=[pl.BlockSpec((tt * PEER_PICKS,), lambda i: (i,), memory_space=pltpu.SMEM),
                  pl.BlockSpec((tt, PEER_PICKS), lambda i: (i, 0)),
                  pl.BlockSpec((tt, SUBLANES, LANES), lambda i: (i, 0, 0)),
                  pl.BlockSpec((1, SUBLANES, LANES), lambda i: (i // steps_per_batch, 0, 0)),
                  pl.BlockSpec(tab.shape, lambda i: (0, 0, 0), pipeline_mode=pl.Buffered(1))],
        out_specs=pl.BlockSpec((tt, SUBLANES, LANES), lambda i: (i, 0, 0)),
        out_shape=jax.ShapeDtypeStruct((t, SUBLANES, LANES), F32),
        scratch_shapes=[pltpu.VMEM((tt * PEER_PICKS, LANES), F32)],
        compiler_params=pltpu.CompilerParams(dimension_semantics=("arbitrary",),
                                             vmem_limit_bytes=VMEM_LIMIT_BYTES),
        name="peer_out",
    )(code, act, x1, gate2, tab)


def _peer_ffn_residual(x1, h2, q, gate2, sub_keys, peer_u, peer_v):
    bsz, s, d = x1.shape
    t = bsz * s
    code, gw = _route_call(q.reshape(t, -1), sub_keys.astype(BF16))
    code = code.reshape(-1)
    act = _peer_act_call(code, h2.reshape(t, SUBLANES, LANES), gw, _tile_table(peer_u))
    out = _peer_out_call(code, act, x1.reshape(t, SUBLANES, LANES),
                         gate2.reshape(bsz, SUBLANES, LANES), _tile_table(peer_v), s)
    return out.reshape(bsz, s, d)


def kernel(x, c, w_ada, b_ada, norm1_g, w_in, mu_rwkv, w0, w2_decay, a0, a2_iclr, g2_gate, k_k, k_a,
           r_k, lnx_w, lnx_b, q_norm_g, k_norm_g, w_branch, w_out, norm2_g, w_peer_q, peer_sub_keys,
           peer_u, peer_v):
    bsz, s, d = x.shape
    ada = _ada_call(c, w_ada, b_ada).reshape(bsz, 6, 1, d)
    shift1, scale1, gate1, shift2, scale2, gate2 = (ada[:, j] for j in range(6))
    w_r = _rwkv_weight_cols(w_in[:, 0:1824]).astype(BF16)
    w_q, w_k, w_v = (w_in[:, 1824 + j * WIDTH:1824 + (j + 1) * WIDTH].astype(BF16) for j in range(3))
    w_g = w_in[:, 1824 + 3 * WIDTH:].astype(BF16)
    p_r, q, k, v, gates = _proj_call(x, norm1_g, scale1, shift1, w_r, w_q, w_k, w_v, w_g)
    y_a = _rwkv_call(p_r, *_rwkv_params(mu_rwkv, w0, w2_decay, a0, a2_iclr, g2_gate, k_k, k_a, r_k,
                                        lnx_w, lnx_b))
    y_bt = _moba_branch(q, k, v, q_norm_g, k_norm_g)
    x1, h2, pq = _merge_call(x, y_a, y_bt, gates, w_branch[0].astype(BF16), w_branch[1].astype(BF16),
                             w_out.astype(BF16), gate1, norm2_g, scale2, shift2, w_peer_q.astype(BF16))
    return _peer_ffn_residual(x1, h2, pq, gate2, peer_sub_keys, peer_u, peer_v)
```

```python
import functools

import jax
import jax.numpy as jnp
from jax import lax
from jax.experimental import pallas as pl
from jax.experimental.pallas import tpu as pltpu

F32 = jnp.float32
BF16 = jnp.bfloat16
I32 = jnp.int32

LANES = 128
SUBLANES = 8
VMEM_LIMIT_BYTES = 56 * 1024 * 1024

NORM_EPS = 1e-6
LNX_EPS = 64e-5
NEG_INF = -1e30

HEAD_DIM = 64
N_HEADS = 8
WIDTH = N_HEADS * HEAD_DIM
RWKV_CHUNK = 64
RWKV_TILE = 128
RWKV_PCOLS = 2048
MOBA_BLOCK = 256
MOBA_TOPK = 3
PEER_HEADS = 8
PEER_KEYS = 128
PEER_TOPK = 16
PEER_PICKS = PEER_HEADS * PEER_TOPK


def _mm(a, b):
    return jnp.dot(a.astype(BF16), b.astype(BF16), preferred_element_type=F32)


def _mm_nt(a, b):
    return lax.dot_general(a.astype(BF16), b.astype(BF16), (((1,), (1,)), ((), ())),
                           preferred_element_type=F32)


def _mm_tn(a, b):
    return lax.dot_general(a.astype(BF16), b.astype(BF16), (((0,), (0,)), ((), ())),
                           preferred_element_type=F32)


def _mm_split(a, b):
    hi = a.astype(BF16)
    lo = (a - hi.astype(F32)).astype(BF16)
    b = b.astype(BF16)
    return (jnp.dot(hi, b, preferred_element_type=F32) + jnp.dot(lo, b, preferred_element_type=F32))


def _mmh(a, b):
    return jnp.dot(a, b, preferred_element_type=F32, precision=lax.Precision.HIGHEST)


def _head_ones(n):
    r = lax.broadcasted_iota(I32, (n, n), 0) // HEAD_DIM
    c = lax.broadcasted_iota(I32, (n, n), 1) // HEAD_DIM
    return (r == c).astype(F32)


def _sigmoid(x):
    return 1.0 / (1.0 + jnp.exp(-x))


def _ada_kernel(c_ref, w_ref, b_ref, o_ref):
    c = c_ref[...]
    o_ref[...] = _mmh(c * _sigmoid(c), w_ref[...]) + b_ref[...]


def _ada_call(c, w_ada, b_ada):
    bsz, d = c.shape
    n = w_ada.shape[1]
    tn = 1536
    return pl.pallas_call(
        _ada_kernel,
        grid=(n // tn,),
        in_specs=[pl.BlockSpec((bsz, d), lambda j: (0, 0)),
                  pl.BlockSpec((d, tn), lambda j: (0, j)),
                  pl.BlockSpec((1, tn), lambda j: (0, j))],
        out_specs=pl.BlockSpec((bsz, tn), lambda j: (0, j)),
        out_shape=jax.ShapeDtypeStruct((bsz, n), F32),
        compiler_params=pltpu.CompilerParams(dimension_semantics=("arbitrary",),
                                             vmem_limit_bytes=VMEM_LIMIT_BYTES),
        name="ada",
    )(c, w_ada, b_ada.reshape(1, n))


def _rwkv_kernel(p_ref, pprev_ref, mu_ref, vec_ref, w2_ref, a2_ref, g2_ref, y_ref, st_ref):
    i = pl.program_id(1)
    tile = p_ref.shape[1]
    L = RWKV_CHUNK

    @pl.when(i == 0)
    def _():
        st_ref[...] = jnp.zeros_like(st_ref)

    p = p_ref[0]
    prev_last = jnp.where(i == 0, 0.0, pprev_ref[0][SUBLANES - 1:SUBLANES, :])
    row = lax.broadcasted_iota(I32, p.shape, 0)
    p_prev = jnp.where(row == 0, prev_last, pltpu.roll(p, 1, 0))
    xm = p + (p_prev - p) * mu_ref[...]
    r = xm[:, 0:WIDTH]
    k = xm[:, WIDTH:2 * WIDTH]
    v = xm[:, 2 * WIDTH:3 * WIDTH]
    wl = xm[:, 3 * WIDTH:3 * WIDTH + 128]
    al = xm[:, 3 * WIDTH + 128:3 * WIDTH + 256]
    gl = xm[:, 3 * WIDTH + 256:3 * WIDTH + 512]
    w0, a0, k_k, k_a = vec_ref[0:1, :], vec_ref[1:2, :], vec_ref[2:3, :], vec_ref[3:4, :]
    r_k, lnx_w, lnx_b = vec_ref[4:5, :], vec_ref[5:6, :], vec_ref[6:7, :]

    z = -(w0 + _mmh(jnp.tanh(wl), w2_ref[...]))
    softplus = jnp.maximum(z, 0.0) + jnp.log(1.0 + jnp.exp(-jnp.abs(z)))
    lw = -jnp.exp(-softplus - 0.5)
    a = _sigmoid(a0 + _mmh(al, a2_ref[...]))
    g = _mmh(_sigmoid(gl), g2_ref[...])
    hones = _head_ones(WIDTH)
    kk = k * k_k
    kk = kk / jnp.maximum(jnp.sqrt(_mm(kk * kk, hones)), 1e-12)
    k2 = k * (1.0 + (a - 1.0) * k_a)
    a_s = -kk
    b_s = kk * a

    li = lax.broadcasted_iota(I32, (L, L), 0)
    lj = lax.broadcasted_iota(I32, (L, L), 1)
    tril_incl = li >= lj
    tril_strict = li > lj
    eye = (li == lj).astype(F32)
    csum = tril_incl.astype(F32)

    n_chunks = tile // L
    items = [(c, h) for c in range(n_chunks) for h in range(N_HEADS)]
    zeros_ll = jnp.zeros((L, L), F32)
    per_chunk = []
    for c in range(n_chunks):
        sl = slice(c * L, (c + 1) * L)
        lw_c = lw[sl]
        cum = _mmh(csum, lw_c)
        ref = cum[L // 2 - 1:L // 2, :]
        g_inv = jnp.exp(ref - cum)
        g_end = jnp.exp(cum[L - 1:L, :] - ref)
        bt = b_s[sl] * g_inv
        kt = k2[sl] * g_inv
        per_chunk.append(dict(
            at=a_s[sl] * jnp.exp(cum - lw_c - ref), rt=r[sl] * jnp.exp(cum - ref), bt=bt, kt=kt,
            bbar=bt * g_end, kbar=kt * g_end, v=v[sl], g_tot=jnp.exp(cum[L - 1:L, :]), g_ref=jnp.exp(ref)))

    def head(c, h, name):
        return per_chunk[c][name][:, h * HEAD_DIM:(h + 1) * HEAD_DIM]

    sc = [_mm_nt(jnp.concatenate([head(c, h, "at"), head(c, h, "rt")], axis=0),
                 jnp.concatenate([head(c, h, "bt"), head(c, h, "kt")], axis=0)) for c, h in items]
    m_ab = [jnp.where(tril_strict, s[0:L, 0:L], 0.0) for s in sc]
    m_ak = [jnp.where(tril_strict, s[0:L, L:2 * L], 0.0) for s in sc]
    tril_incl2 = (lax.broadcasted_iota(I32, (L, 2 * L), 0)
                  >= lax.broadcasted_iota(I32, (L, 2 * L), 1) % L)
    g_rbk = [jnp.where(tril_incl2, s[L:2 * L, :], 0.0) for s in sc]
    akv = [_mm(m, head(c, h, "v")) for m, (c, h) in zip(m_ak, items)]
    pw = [_mm(m, m) for m in m_ab]
    t_inv = [eye + m for m in m_ab]
    for _ in range(4):
        both = [_mm(jnp.concatenate([p, t], axis=0), p) for p, t in zip(pw, t_inv)]
        t_inv = [t + b[L:2 * L] for t, b in zip(t_inv, both)]
        pw = [b[0:L] for b in both]
    t_inv = [t + _mm(t, p) for t, p in zip(t_inv, pw)]
    tw = [_mm(t, jnp.concatenate([head(c, h, "at"), x], axis=1)) for t, x, (c, h) in zip(t_inv, akv, items)]
    ro = [_mm(g, jnp.concatenate([w, jnp.concatenate([zeros_ll, head(c, h, "v")], axis=1)], axis=0))
          for g, w, (c, h) in zip(g_rbk, tw, items)]
    a_hat = [w[:, 0:L] * head(c, h, "g_ref") for w, (c, h) in zip(tw, items)]
    r_hat = [(head(c, h, "rt") + x[:, 0:L]) * head(c, h, "g_ref") for x, (c, h) in zip(ro, items)]
    pp = [_mm_tn(jnp.concatenate([head(c, h, "bbar"), head(c, h, "kbar")], axis=0),
                 jnp.concatenate([jnp.concatenate([a, w[:, L:2 * L]], axis=1),
                                  jnp.concatenate([zeros_ll, head(c, h, "v")], axis=1)], axis=0))
          for a, w, (c, h) in zip(a_hat, tw, items)]
    state = [st_ref[h] for h in range(N_HEADS)]
    outs = []
    for c in range(n_chunks):
        head_outs = []
        for h in range(N_HEADS):
            i = c * N_HEADS + h
            phi_t = eye * head(c, h, "g_tot") + pp[i][:, 0:L]
            both = _mm(jnp.concatenate([r_hat[i], phi_t], axis=0), state[h])
            head_outs.append(both[0:L] + ro[i][:, L:2 * L])
            state[h] = both[L:2 * L] + pp[i][:, L:2 * L]
        outs.append(jnp.concatenate(head_outs, axis=1))
    for h in range(N_HEADS):
        st_ref[h] = state[h]
    out = jnp.concatenate(outs, axis=0)

    inv_n = 1.0 / HEAD_DIM
    mean = _mm(out, hones) * inv_n
    cen = out - mean
    var = _mm(cen * cen, hones) * inv_n
    out = cen * lax.rsqrt(var + LNX_EPS) * lnx_w + lnx_b
    bonus = _mm(r * k2 * r_k, hones) * v
    y_ref[0] = (out + bonus) * g


def _mmh_nt(a, b):
    return lax.dot_general(a, b, (((1,), (1,)), ((), ())), preferred_element_type=F32,
                           precision=lax.Precision.HIGHEST)


def _mmh_tn(a, b):
    return lax.dot_general(a, b, (((0,), (0,)), ((), ())), preferred_element_type=F32,
                           precision=lax.Precision.HIGHEST)


def _rwkv_call(p, mu, vecs, w2, a2, g2):
    bsz, s, pc = p.shape
    tile = RWKV_TILE
    const = lambda b, i: (0, 0)
    return pl.pallas_call(
        _rwkv_kernel,
        grid=(bsz, s // tile),
        in_specs=[pl.BlockSpec((1, tile, pc), lambda b, i: (b, i, 0)),
                  pl.BlockSpec((1, SUBLANES, pc),
                               lambda b, i: (b, jnp.maximum(i * (tile // SUBLANES) - 1, 0), 0)),
                  pl.BlockSpec((1, pc), const),
                  pl.BlockSpec((SUBLANES, WIDTH), const),
                  pl.BlockSpec((128, WIDTH), const),
                  pl.BlockSpec((128, WIDTH), const),
                  pl.BlockSpec((256, WIDTH), const)],
        out_specs=pl.BlockSpec((1, tile, WIDTH), lambda b, i: (b, i, 0)),
        out_shape=jax.ShapeDtypeStruct((bsz, s, WIDTH), F32),
        scratch_shapes=[pltpu.VMEM((N_HEADS, HEAD_DIM, HEAD_DIM), F32)],
        compiler_params=pltpu.CompilerParams(dimension_semantics=("arbitrary", "arbitrary"),
                                             vmem_limit_bytes=VMEM_LIMIT_BYTES),
        name="rwkv",
    )(p, p, mu, vecs, w2, a2, g2)


def _pad_rows(w, n):
    return jnp.pad(w, ((0, n - w.shape[0]), (0, 0)))


def _rwkv_params(mu_rwkv, w0, w2_decay, a0, a2_iclr, g2_gate, k_k, k_a, r_k, lnx_w, lnx_b):
    mu = jnp.concatenate([mu_rwkv[0:1536], _pad1(mu_rwkv[1536:1600], 128),
                          _pad1(mu_rwkv[1600:1664], 128), _pad1(mu_rwkv[1664:1824], 256)])
    vecs = jnp.stack([w0, a0, k_k, k_a, r_k.reshape(-1), lnx_w, lnx_b, jnp.zeros_like(w0)])
    return (mu.reshape(1, RWKV_PCOLS), vecs, _pad_rows(w2_decay, 128), _pad_rows(a2_iclr, 128),
            _pad_rows(g2_gate, 256))


def _pad1(v, n):
    return jnp.pad(v, (0, n - v.shape[0]))


def _pad_last(x, n):
    return jnp.pad(x, [(0, 0)] * (x.ndim - 1) + [(0, n - x.shape[-1])])


def _pad_cols(w, n):
    return _pad_last(w, n)


def _rwkv_weight_cols(w):
    return jnp.concatenate([w[:, 0:1536], _pad_cols(w[:, 1536:1600], 128),
                            _pad_cols(w[:, 1600:1664], 128), _pad_cols(w[:, 1664:1824], 256)], axis=1)


def _modulated_norm(x, gain, scale, shift):
    y = x * lax.rsqrt(jnp.mean(x * x, axis=-1, keepdims=True) + NORM_EPS)
    return y * gain * (1.0 + scale) + shift


def _proj_kernel(x_ref, g_ref, sc_ref, sh_ref, wr_ref, wq_ref, wk_ref, wv_ref, wg_ref,
                 pr_ref, q_ref, k_ref, v_ref, gate_ref):
    h = _modulated_norm(x_ref[0], g_ref[...], sc_ref[0], sh_ref[0]).astype(BF16)
    pr_ref[0] = jnp.dot(h, wr_ref[...], preferred_element_type=F32)
    q_ref[0] = jnp.dot(h, wq_ref[...], preferred_element_type=F32)
    k_ref[0] = jnp.dot(h, wk_ref[...], preferred_element_type=F32)
    v_ref[0] = jnp.dot(h, wv_ref[...], preferred_element_type=F32)
    gate_ref[0] = _sigmoid(jnp.dot(h, wg_ref[...], preferred_element_type=F32))


def _proj_call(x, norm_g, scale, shift, w_r, w_q, w_k, w_v, w_g):
    bsz, s, d = x.shape
    tm = 512
    tok = lambda b, i: (b, i, 0)
    per_b = lambda b, i: (b, 0, 0)
    const = lambda b, i: (0, 0)
    outs = [(RWKV_PCOLS, F32), (WIDTH, F32), (WIDTH, F32), (WIDTH, F32), (2 * d, F32)]
    return pl.pallas_call(
        _proj_kernel,
        grid=(bsz, s // tm),
        in_specs=[pl.BlockSpec((1, tm, d), tok),
                  pl.BlockSpec((1, d), const),
                  pl.BlockSpec((1, 1, d), per_b),
                  pl.BlockSpec((1, 1, d), per_b)]
                 + [pl.BlockSpec(w.shape, const) for w in (w_r, w_q, w_k, w_v, w_g)],
        out_specs=[pl.BlockSpec((1, tm, n), tok) for n, _ in outs],
        out_shape=[jax.ShapeDtypeStruct((bsz, s, n), dt) for n, dt in outs],
        compiler_params=pltpu.CompilerParams(dimension_semantics=("arbitrary", "arbitrary"),
                                             vmem_limit_bytes=VMEM_LIMIT_BYTES),
        name="proj",
    )(x, norm_g.reshape(1, d), scale, shift, w_r, w_q, w_k, w_v, w_g)


MOBA_PREP_TILE = 2 * MOBA_BLOCK


def _moba_prep_kernel(q_ref, k_ref, v_ref, qg_ref, kg_ref, qt_ref, kn_ref, vt_ref, sel_ref, kmean_ref):
    i = pl.program_id(1)
    tile = q_ref.shape[1]
    nb = kmean_ref.shape[0]
    blocks_per_tile = tile // MOBA_BLOCK

    @pl.when(i == 0)
    def _():
        kmean_ref[...] = jnp.zeros_like(kmean_ref)

    q, k, v = q_ref[0], k_ref[0], v_ref[0]
    hones = _head_ones(WIDTH)
    inv_n = 1.0 / HEAD_DIM
    qn = q * lax.rsqrt(_mm_split(q * q, hones) * inv_n + NORM_EPS) * qg_ref[...] * (HEAD_DIM ** -0.5)
    kn = k * lax.rsqrt(_mm_split(k * k, hones) * inv_n + NORM_EPS) * kg_ref[...]
    kn_ref[0] = kn.astype(BF16)

    kmean = kmean_ref[...]
    brow = lax.broadcasted_iota(I32, kmean.shape, 0)
    for blk in range(blocks_per_tile):
        km = jnp.mean(kn[blk * MOBA_BLOCK:(blk + 1) * MOBA_BLOCK], axis=0, keepdims=True)
        kmean = jnp.where(brow == i * blocks_per_tile + blk, km, kmean)
    kmean_ref[...] = kmean

    qnt = qn.T
    vt = v.T
    half = lax.broadcasted_iota(I32, (2 * HEAD_DIM, tile), 0) // HEAD_DIM
    n_iota = lax.broadcasted_iota(I32, (nb, tile), 0)
    q_blk = i * blocks_per_tile + lax.broadcasted_iota(I32, (nb, tile), 1) // MOBA_BLOCK
    for h in range(N_HEADS):
        pair = qnt[(h // 2) * 2 * HEAD_DIM:(h // 2 + 1) * 2 * HEAD_DIM]
        qt_ref[0, h] = jnp.where(half == h % 2, pair, 0.0).astype(BF16)
        for blk in range(blocks_per_tile):
            vt_ref[0, h, blk] = vt[h * HEAD_DIM:(h + 1) * HEAD_DIM,
                                   blk * MOBA_BLOCK:(blk + 1) * MOBA_BLOCK].astype(BF16)
        hs = slice(h * HEAD_DIM, (h + 1) * HEAD_DIM)
        gate = _mmh_nt(kmean[:, hs], qn[:, hs])
        work = jnp.where(n_iota < q_blk, gate, -jnp.inf)
        bits = jnp.zeros((1, tile), I32)
        for _ in range(MOBA_TOPK):
            m = jnp.max(work, axis=0, keepdims=True)
            idx = jnp.min(jnp.where(work == m, n_iota, nb), axis=0, keepdims=True)
            hit = n_iota == idx
            take = jnp.logical_and(hit, m > -jnp.inf)
            bits = bits | jnp.sum(jnp.where(take, jnp.left_shift(1, n_iota), 0), axis=0, keepdims=True)
            work = jnp.where(hit, -jnp.inf, work)
        sel_ref[0, h] = bits


def _moba_prep_call(q, k, v, q_norm_g, k_norm_g):
    bsz, s, w = q.shape
    tile = MOBA_PREP_TILE
    nb = s // MOBA_BLOCK
    assert nb <= 32, "block selection is a 32-bit mask"
    tok = lambda b, i: (b, i, 0)
    const = lambda b, i: (0, 0)
    return pl.pallas_call(
        _moba_prep_kernel,
        grid=(bsz, s // tile),
        in_specs=[pl.BlockSpec((1, tile, w), tok)] * 3 + [pl.BlockSpec((1, w), const)] * 2,
        out_specs=[pl.BlockSpec((1, N_HEADS, 2 * HEAD_DIM, tile), lambda b, i: (b, 0, 0, i)),
                   pl.BlockSpec((1, tile, w), tok),
                   pl.BlockSpec((1, N_HEADS, tile // MOBA_BLOCK, HEAD_DIM, MOBA_BLOCK),
                                lambda b, i: (b, 0, i, 0, 0)),
                   pl.BlockSpec((1, N_HEADS, 1, tile), lambda b, i: (b, 0, 0, i))],
        out_shape=[jax.ShapeDtypeStruct((bsz, N_HEADS, 2 * HEAD_DIM, s), BF16),
                   jax.ShapeDtypeStruct((bsz, s, w), BF16),
                   jax.ShapeDtypeStruct((bsz, N_HEADS, nb, HEAD_DIM, MOBA_BLOCK), BF16),
                   jax.ShapeDtypeStruct((bsz, N_HEADS, 1, s), I32)],
        scratch_shapes=[pltpu.VMEM((nb, w), F32)],
        compiler_params=pltpu.CompilerParams(dimension_semantics=("arbitrary", "arbitrary"),
                                             vmem_limit_bytes=VMEM_LIMIT_BYTES),
        name="moba_prep",
    )(q, k, v, jnp.tile(q_norm_g, N_HEADS).reshape(1, w), jnp.tile(k_norm_g, N_HEADS).reshape(1, w))


MOBA_KV_UNROLL = 4


def _moba_attn_kernel(qt_ref, k_ref, vt_ref, sel_ref, o_ref):
    qb = pl.program_id(2)
    bl = MOBA_BLOCK
    nb = vt_ref.shape[2]
    kpos = lax.broadcasted_iota(I32, (bl, bl), 0)
    qpos = lax.broadcasted_iota(I32, (bl, bl), 1)

    def key_block(j):
        return k_ref[0, pl.ds(pl.multiple_of(j * bl, bl), bl), :]

    k_own = key_block(qb)
    state = []
    for e in range(2):
        s = jnp.where(kpos <= qpos, jnp.dot(k_own, qt_ref[0, e], preferred_element_type=F32), NEG_INF)
        m = jnp.max(s, axis=0, keepdims=True)
        p = jnp.exp(s - m)
        l = jnp.sum(p, axis=0, keepdims=True)
        acc = jnp.dot(vt_ref[0, e, qb], p.astype(BF16), preferred_element_type=F32)
        state += [m, l, acc]

    def body(it, carry):
        carry = list(carry)
        js = [it * MOBA_KV_UNROLL + u for u in range(MOBA_KV_UNROLL)]
        jc = [jnp.minimum(j, nb - 1) for j in js]
        ks = [key_block(j) for j in jc]
        raw = [[jnp.dot(kj, qt_ref[0, e], preferred_element_type=F32) for kj in ks] for e in range(2)]
        probs = []
        for e in range(2):
            m, l, acc = carry[3 * e:3 * e + 3]
            sel = sel_ref[0, e]
            ss = []
            for j, j_in, s in zip(js, jc, raw[e]):
                picked = jnp.logical_and((jnp.right_shift(sel, j_in) & 1) != 0, j < qb)
                ss.append(jnp.where(picked, s, NEG_INF))
            m_new = m
            for s in ss:
                m_new = jnp.maximum(m_new, jnp.max(s, axis=0, keepdims=True))
            alpha = jnp.exp(m - m_new)
            ps = [jnp.exp(s - m_new) for s in ss]
            l = alpha * l
            for p in ps:
                l = l + jnp.sum(p, axis=0, keepdims=True)
            probs.append([p.astype(BF16) for p in ps])
            carry[3 * e:3 * e + 3] = [m_new, l, alpha * acc]
        for e in range(2):
            pv = [jnp.dot(vt_ref[0, e, j], p, preferred_element_type=F32) for j, p in zip(jc, probs[e])]
            carry[3 * e + 2] = carry[3 * e + 2] + sum(pv[1:], pv[0])
        return tuple(carry)

    trips = (qb + MOBA_KV_UNROLL - 1) // MOBA_KV_UNROLL
    state = lax.fori_loop(0, trips, body, tuple(state))
    for e in range(2):
        o_ref[0, e] = state[3 * e + 2] / state[3 * e + 1]


def _moba_attn_call(qt, kn, vt, sel):
    bsz, nh, _, s = qt.shape
    nb = s // MOBA_BLOCK
    return pl.pallas_call(
        _moba_attn_kernel,
        grid=(bsz, nh // 2, nb),
        in_specs=[pl.BlockSpec((1, 2, 2 * HEAD_DIM, MOBA_BLOCK), lambda b, h, i: (b, h, 0, i)),
                  pl.BlockSpec((1, s, 2 * HEAD_DIM), lambda b, h, i: (b, 0, h)),
                  pl.BlockSpec((1, 2, nb, HEAD_DIM, MOBA_BLOCK), lambda b, h, i: (b, h, 0, 0, 0)),
                  pl.BlockSpec((1, 2, 1, MOBA_BLOCK), lambda b, h, i: (b, h, 0, i))],
        out_specs=pl.BlockSpec((1, 2, HEAD_DIM, MOBA_BLOCK), lambda b, h, i: (b, h, 0, i)),
        out_shape=jax.ShapeDtypeStruct((bsz, nh, HEAD_DIM, s), F32),
        compiler_params=pltpu.CompilerParams(
            dimension_semantics=("arbitrary", "arbitrary", "arbitrary"),
            vmem_limit_bytes=VMEM_LIMIT_BYTES),
        name="moba_attn",
    )(qt, kn, vt, sel)


def _moba_branch(q, k, v, q_norm_g, k_norm_g):
    qt, kn, vt, sel = _moba_prep_call(q, k, v, q_norm_g, k_norm_g)
    yt = _moba_attn_call(qt, kn, vt, sel)
    return yt.reshape(yt.shape[0], WIDTH, yt.shape[3])


def _merge_kernel(x_ref, ya_ref, ybt_ref, gates_ref, wb0_ref, wb1_ref, wo_ref, g1_ref, n2_ref, sc2_ref,
                  sh2_ref, wq_ref, x1_ref, h2_ref, q_ref):
    d = x_ref.shape[2]
    ys_a = _mm(ya_ref[0], wb0_ref[...])
    ys_b = _mm_tn(ybt_ref[0], wb1_ref[...])
    gates = gates_ref[0]
    mixed = gates[:, 0:d] * ys_a + gates[:, d:2 * d] * ys_b
    x1 = x_ref[0] + g1_ref[0] * _mm(mixed, wo_ref[...])
    x1_ref[0] = x1
    h2 = _modulated_norm(x1, n2_ref[...], sc2_ref[0], sh2_ref[0])
    h2_ref[0] = h2
    q_ref[0] = _mm(h2, wq_ref[...])


def _merge_call(x, y_a, y_bt, gates, w_b0, w_b1, w_out, gate1, norm2_g, scale2, shift2, w_peer_q):
    bsz, s, d = x.shape
    tm = 256
    nq = w_peer_q.shape[1]
    tok = lambda b, i: (b, i, 0)
    per_b = lambda b, i: (b, 0, 0)
    const = lambda b, i: (0, 0)
    return pl.pallas_call(
        _merge_kernel,
        grid=(bsz, s // tm),
        in_specs=[pl.BlockSpec((1, tm, d), tok),
                  pl.BlockSpec((1, tm, WIDTH), tok),
                  pl.BlockSpec((1, WIDTH, tm), lambda b, i: (b, 0, i)),
                  pl.BlockSpec((1, tm, 2 * d), tok),
                  pl.BlockSpec((WIDTH, d), const),
                  pl.BlockSpec((WIDTH, d), const),
                  pl.BlockSpec((d, d), const),
                  pl.BlockSpec((1, 1, d), per_b),
                  pl.BlockSpec((1, d), const),
                  pl.BlockSpec((1, 1, d), per_b),
                  pl.BlockSpec((1, 1, d), per_b),
                  pl.BlockSpec((d, nq), const)],
        out_specs=[pl.BlockSpec((1, tm, d), tok), pl.BlockSpec((1, tm, d), tok),
                   pl.BlockSpec((1, tm, nq), tok)],
        out_shape=[jax.ShapeDtypeStruct((bsz, s, d), F32), jax.ShapeDtypeStruct((bsz, s, d), F32),
                   jax.ShapeDtypeStruct((bsz, s, nq), F32)],
        compiler_params=pltpu.CompilerParams(dimension_semantics=("arbitrary", "arbitrary"),
                                             vmem_limit_bytes=VMEM_LIMIT_BYTES),
        name="merge",
    )(x, y_a, y_bt, gates, w_b0, w_b1, w_out, gate1, norm2_g.reshape(1, d), scale2, shift2, w_peer_q)


def _topk_rows(x, k, payload=None):
    n = x.shape[0]
    iota = lax.broadcasted_iota(I32, x.shape, 0)
    vals, picks = [], []
    work = x
    for _ in range(k):
        m = jnp.max(work, axis=0, keepdims=True)
        idx = jnp.min(jnp.where(work == m, iota, n), axis=0, keepdims=True)
        hit = iota == idx
        vals.append(m)
        picks.append(idx if payload is None else jnp.max(jnp.where(hit, payload, -1), axis=0, keepdims=True))
        work = jnp.where(hit, -jnp.inf, work)
    return jnp.concatenate(vals, axis=0), jnp.concatenate(picks, axis=0)


def _route_kernel(q_ref, keys_ref, code_ref, gw_ref):
    k = PEER_TOPK
    eids, gws = [], []
    for hp in range(PEER_HEADS):
        q1 = q_ref[:, (2 * hp) * PEER_KEYS:(2 * hp + 1) * PEER_KEYS]
        q2 = q_ref[:, (2 * hp + 1) * PEER_KEYS:(2 * hp + 2) * PEER_KEYS]
        s1 = _mm_nt(keys_ref[0, hp], q1)
        s2 = _mm_nt(keys_ref[1, hp], q2)
        v1, i1 = _topk_rows(s1, k)
        v2, i2 = _topk_rows(s2, k)
        widths = [k // (a + 1) for a in range(k)]
        pad = -sum(widths) % SUBLANES
        comb = jnp.concatenate([v1[a:a + 1] + v2[0:w] for a, w in enumerate(widths)]
                               + [jnp.full((pad, v1.shape[1]), -jnp.inf, F32)], axis=0)
        expert = jnp.concatenate([i1[a:a + 1] * PEER_KEYS + i2[0:w] for a, w in enumerate(widths)]
                                 + [jnp.zeros((pad, v1.shape[1]), I32)], axis=0)
        sc, eid = _topk_rows(comb, k, payload=expert)
        e = jnp.exp(sc - sc[0:1])
        gws.append(e / jnp.sum(e, axis=0, keepdims=True))
        eids.append(eid)
    code_ref[...] = jnp.concatenate(eids, axis=0).T
    gw_ref[...] = jnp.concatenate(gws, axis=0).T


def _route_call(q, sub_keys):
    t, nq = q.shape
    tt = 256
    return pl.pallas_call(
        _route_kernel,
        grid=(t // tt,),
        in_specs=[pl.BlockSpec((tt, nq), lambda i: (i, 0)),
                  pl.BlockSpec(sub_keys.shape, lambda i: (0, 0, 0, 0))],
        out_specs=[pl.BlockSpec((tt, PEER_PICKS), lambda i: (i, 0)),
                   pl.BlockSpec((tt, PEER_PICKS), lambda i: (i, 0))],
        out_shape=[jax.ShapeDtypeStruct((t, PEER_PICKS), I32), jax.ShapeDtypeStruct((t, PEER_PICKS), F32)],
        compiler_params=pltpu.CompilerParams(dimension_semantics=("arbitrary",),
                                             vmem_limit_bytes=VMEM_LIMIT_BYTES),
        name="route",
    )(q, sub_keys)


PEER_TOKENS_PER_STEP = 64
PEER_ACCUMULATORS = 4


def _tile_table(tab):
    n, d = tab.shape
    return tab.astype(BF16).reshape(n, d // LANES, LANES)


def _gelu(x):
    return 0.5 * x * (1.0 + lax.erf(x * (2.0 ** -0.5)))


def _sublane_sums(tiles):
    row = lax.broadcasted_iota(I32, (SUBLANES, LANES), 0)
    odd, upper_pair, upper_half = (row & 1) != 0, (row & 2) != 0, row >= 4

    def quad(a, b, c, d):
        ab = jnp.where(odd, a + pltpu.roll(a, 1, 0), b + pltpu.roll(b, SUBLANES - 1, 0))
        cd = jnp.where(odd, c + pltpu.roll(c, 1, 0), d + pltpu.roll(d, SUBLANES - 1, 0))
        q = jnp.where(upper_pair, ab + pltpu.roll(ab, 2, 0), cd + pltpu.roll(cd, SUBLANES - 2, 0))
        return q + pltpu.roll(q, 4, 0)

    return jnp.where(upper_half, quad(tiles[7], tiles[6], tiles[5], tiles[4]),
                     quad(tiles[3], tiles[2], tiles[1], tiles[0]))


def _peer_act_kernel(code_ref, h_ref, gw_ref, tab_ref, act_ref, psum_ref):
    tokens = h_ref.shape[0]

    def token(t, carry):
        h = h_ref[t]
        base = pl.multiple_of(t * PEER_PICKS, PEER_PICKS)
        for j0 in range(0, PEER_PICKS, SUBLANES):
            prods = [tab_ref[code_ref[base + j0 + j]].astype(F32) * h for j in range(SUBLANES)]
            psum_ref[pl.ds(base + j0, SUBLANES), :] = _sublane_sums(prods)
        return carry

    lax.fori_loop(0, tokens, token, 0)
    ones = jnp.ones((SUBLANES, LANES), BF16)
    dots = jnp.concatenate(
        [_mm_nt(ones, psum_ref[t * PEER_PICKS:(t + 1) * PEER_PICKS, :])[0:1] for t in range(tokens)], axis=0)
    act_ref[...] = _gelu(dots) * gw_ref[...]


def _peer_act_call(code, h2, gw, tab):
    t = h2.shape[0]
    tt = PEER_TOKENS_PER_STEP
    return pl.pallas_call(
        _peer_act_kernel,
        grid=(t // tt,),
        in_specs=[pl.BlockSpec((tt * PEER_PICKS,), lambda i: (i,), memory_space=pltpu.SMEM),
                  pl.BlockSpec((tt, SUBLANES, LANES), lambda i: (i, 0, 0)),
                  pl.BlockSpec((tt, PEER_PICKS), lambda i: (i, 0)),
                  pl.BlockSpec(tab.shape, lambda i: (0, 0, 0), pipeline_mode=pl.Buffered(1))],
        out_specs=pl.BlockSpec((tt, PEER_PICKS), lambda i: (i, 0)),
        out_shape=jax.ShapeDtypeStruct((t, PEER_PICKS), F32),
        scratch_shapes=[pltpu.VMEM((tt * PEER_PICKS, LANES), F32)],
        compiler_params=pltpu.CompilerParams(dimension_semantics=("arbitrary",),
                                             vmem_limit_bytes=VMEM_LIMIT_BYTES),
        name="peer_act",
    )(code, h2, gw, tab)


def _peer_out_kernel(code_ref, act_ref, x_ref, g2_ref, tab_ref, o_ref, wrep_ref):
    tokens = x_ref.shape[0]
    g2 = g2_ref[0]
    eye = (lax.broadcasted_iota(I32, (PEER_PICKS, PEER_PICKS), 0)
           == lax.broadcasted_iota(I32, (PEER_PICKS, PEER_PICKS), 1)).astype(F32)
    ones = jnp.ones((PEER_PICKS, LANES), BF16)
    for t in range(tokens):
        diag = eye * act_ref[t:t + 1, :]
        if t % 2:
            rep = jnp.broadcast_to(jnp.sum(diag, axis=1, keepdims=True), (PEER_PICKS, LANES))
        else:
            rep = _mm_split(diag, ones)
        wrep_ref[t * PEER_PICKS:(t + 1) * PEER_PICKS, :] = rep

    def token(t, carry):
        groups = 4
        per = PEER_PICKS // groups

        def group(g, accs):
            accs = list(accs)
            base = pl.multiple_of(t * PEER_PICKS + g * per, per)
            for j in range(per):
                w = jnp.broadcast_to(wrep_ref[pl.ds(base + j, 1), :], (SUBLANES, LANES))
                accs[j % PEER_ACCUMULATORS] += w * tab_ref[code_ref[base + j]].astype(F32)
            return tuple(accs)

        zero = jnp.zeros((SUBLANES, LANES), F32)
        accs = lax.fori_loop(0, groups, group, (zero,) * PEER_ACCUMULATORS)
        o_ref[t] = x_ref[t] + g2 * ((accs[0] + accs[1]) + (accs[2] + accs[3]))
        return carry

    lax.fori_loop(0, tokens, token, 0)


def _peer_out_call(code, act, x1, gate2, tab, tokens_per_batch):
    t = x1.shape[0]
    tt = PEER_TOKENS_PER_STEP
    steps_per_batch = tokens_per_batch // tt
    return pl.pallas_call(
        _peer_out_kernel,
        grid=(t // tt,),
        in_specs=[pl.BlockSpec((tt * PEER_PICKS,), lambda i: (i,), memory_space=pltpu.SMEM),
                  pl.BlockSpec((tt, PEER_PICKS), lambda i: (i, 0)),
                  pl.BlockSpec((tt, SUBLANES, LANES), lambda i: (i, 0, 0)),
                  pl.BlockSpec((1, SUBLANES, LANES), lambda i: (i // steps_per_batch, 0, 0)),
                  pl.BlockSpec(tab.shape, lambda i: (0, 0, 0), pipeline_mode=pl.Buffered(1))],
        out_specs=pl.BlockSpec((tt, SUBLANES, LANES), lambda i: (i, 0, 0)),
        out_shape=jax.ShapeDtypeStruct((t, SUBLANES, LANES), F32),
        scratch_shapes=[pltpu.VMEM((tt * PEER_PICKS, LANES), F32)],
        compiler_params=pltpu.CompilerParams(dimension_semantics=("arbitrary",),
                                             vmem_limit_bytes=VMEM_LIMIT_BYTES),
        name="peer_out",
    )(code, act, x1, gate2, tab)


def _peer_ffn_residual(x1, h2, q, gate2, sub_keys, peer_u, peer_v):
    bsz, s, d = x1.shape
    t = bsz * s
    code, gw = _route_call(q.reshape(t, -1), sub_keys.astype(BF16))
    code = code.reshape(-1)
    act = _peer_act_call(code, h2.reshape(t, SUBLANES, LANES), gw, _tile_table(peer_u))
    out = _peer_out_call(code, act, x1.reshape(t, SUBLANES, LANES),
                         gate2.reshape(bsz, SUBLANES, LANES), _tile_table(peer_v), s)
    return out.reshape(bsz, s, d)


def kernel(x, c, w_ada, b_ada, norm1_g, w_in, mu_rwkv, w0, w2_decay, a0, a2_iclr, g2_gate, k_k, k_a,
           r_k, lnx_w, lnx_b, q_norm_g, k_norm_g, w_branch, w_out, norm2_g, w_peer_q, peer_sub_keys,
           peer_u, peer_v):
    bsz, s, d = x.shape
    ada = _ada_call(c, w_ada, b_ada).reshape(bsz, 6, 1, d)
    shift1, scale1, gate1, shift2, scale2, gate2 = (ada[:, j] for j in range(6))
    w_r = _rwkv_weight_cols(w_in[:, 0:1824]).astype(BF16)
    w_q, w_k, w_v = (w_in[:, 1824 + j * WIDTH:1824 + (j + 1) * WIDTH].astype(BF16) for j in range(3))
    w_g = w_in[:, 1824 + 3 * WIDTH:].astype(BF16)
    p_r, q, k, v, gates = _proj_call(x, norm1_g, scale1, shift1, w_r, w_q, w_k, w_v, w_g)
    y_a = _rwkv_call(p_r, *_rwkv_params(mu_rwkv, w0, w2_decay, a0, a2_iclr, g2_gate, k_k, k_a, r_k,
                                        lnx_w, lnx_b))
    y_bt = _moba_branch(q, k, v, q_norm_g, k_norm_g)
    x1, h2, pq = _merge_call(x, y_a, y_bt, gates, w_branch[0].astype(BF16), w_branch[1].astype(BF16),
                             w_out.astype(BF16), gate1, norm2_g, scale2, shift2, w_peer_q.astype(BF16))
    return _peer_ffn_residual(x1, h2, pq, gate2, peer_sub_keys, peer_u, peer_v)
```
